```python
import math
import jax, jax.numpy as jnp
from jax import lax
import numpy as np

D_MODEL = 1024
BATCH = 4
SEQ = 4096
DEPTH = 1

GLA_HEADS = 4
GLA_DK = 64
GLA_DV = 128
GLA_RANK = 16
GLA_TAU = 16.0
GLA_CHUNK = 64
MOBA_HEADS = 8
MOBA_HD = 64
MOBA_BLOCK = 256
MOBA_TOPK = 3
MOBA_QCHUNK = 64
ROPE_THETA = 10000.0
MIX_WIDTH = GLA_HEADS * GLA_DV + MOBA_HEADS * MOBA_HD
IN_SPLITS = (GLA_HEADS * GLA_DK, GLA_HEADS * GLA_DK, GLA_HEADS * GLA_DV, GLA_HEADS * GLA_DV,
             GLA_RANK, MOBA_HEADS * MOBA_HD, MOBA_HEADS * MOBA_HD, MOBA_HEADS * MOBA_HD)
IN_WIDTH = sum(IN_SPLITS)
PEER_HEADS = 8
PEER_NKEYS = 128
PEER_N = PEER_NKEYS * PEER_NKEYS
PEER_QDIM = 256
PEER_TOPK = 16
PEER_TOKCHUNK = 128
EPS = 1e-6

kernel_name = "hymba_gla_moba_peer_layer"


def rmsnorm(x, w):
    xf = x.astype(jnp.float32)
    y = xf * lax.rsqrt(jnp.mean(xf * xf, axis=-1, keepdims=True) + EPS)
    return (y * w.astype(jnp.float32)).astype(x.dtype)


def rope(x, pos):
    hd = x.shape[-1]
    half = hd // 2
    inv = ROPE_THETA ** (-jnp.arange(half, dtype=jnp.float32) / half)
    ang = pos.astype(jnp.float32)[:, None] * inv[None, :]
    cos, sin = jnp.cos(ang), jnp.sin(ang)
    xf = x.astype(jnp.float32)
    x1, x2 = xf[..., :half], xf[..., half:]
    return jnp.concatenate([x1 * cos - x2 * sin, x2 * cos + x1 * sin], axis=-1).astype(x.dtype)


def to_heads(t, h):
    b, s, _ = t.shape
    return t.reshape(b, s, h, -1).transpose(0, 2, 1, 3)


def from_heads(t):
    b, h, s, d = t.shape
    return t.transpose(0, 2, 1, 3).reshape(b, s, h * d)


def gla_chunked(q, k, v, log_a):
    B, H, S, dk = q.shape
    dv = v.shape[-1]
    C = GLA_CHUNK
    nc = S // C
    q, k, g = (t.reshape(B, H, nc, C, dk) for t in (q, k, log_a))
    v = v.reshape(B, H, nc, C, dv)
    b = jnp.cumsum(g, axis=3)
    b_last = b[:, :, :, -1:, :]
    q_d = q * jnp.exp(b) * (dk ** -0.5)
    k_in = k * jnp.exp(-b)
    k_st = k * jnp.exp(b_last - b)
    causal = jnp.tril(jnp.ones((C, C), dtype=bool))
    A = jnp.where(causal, jnp.einsum('bhnid,bhnjd->bhnij', q_d, k_in), 0.0)
    o_intra = jnp.einsum('bhnij,bhnje->bhnie', A, v)
    upd = jnp.einsum('bhncd,bhnce->bhnde', k_st, v)
    decay = jnp.exp(b_last[:, :, :, 0, :])

    def step(state, inp):
        dec, u = inp
        return dec[..., None] * state + u, state

    init = jnp.zeros((B, H, dk, dv), jnp.float32)
    _, s_before = lax.scan(step, init, (jnp.moveaxis(decay, 2, 0), jnp.moveaxis(upd, 2, 0)))
    s_before = jnp.moveaxis(s_before, 0, 2)
    o_inter = jnp.einsum('bhncd,bhnde->bhnce', q_d, s_before)
    return (o_intra + o_inter).reshape(B, H, S, dv)


def moba_attention(q, k, v):
    B, H, S, hd = q.shape
    Sp = ((S + MOBA_BLOCK - 1) // MOBA_BLOCK) * MOBA_BLOCK
    padw = ((0, 0), (0, 0), (0, Sp - S), (0, 0))
    q, k, v = (jnp.pad(t, padw) for t in (q, k, v))
    nb = Sp // MOBA_BLOCK
    K = min(MOBA_TOPK, nb)
    kb = k.reshape(B, H, nb, MOBA_BLOCK, hd)
    vb = v.reshape(B, H, nb, MOBA_BLOCK, hd)
    kbar = jnp.mean(kb.astype(jnp.float32), axis=3)
    scale = hd ** -0.5
    bi = jnp.arange(B)[:, None, None, None]
    hi = jnp.arange(H)[None, :, None, None]
    nq = Sp // MOBA_QCHUNK

    def one_chunk(c):
        q0 = c * MOBA_QCHUNK
        qc = lax.dynamic_slice_in_dim(q, q0, MOBA_QCHUNK, axis=2)
        blk = q0 // MOBA_BLOCK
        bscore = jnp.einsum('bhqd,bhnd->bhqn', qc.astype(jnp.float32), kbar)
        bscore = jnp.where(jnp.arange(nb) < blk, bscore, -jnp.inf)
        _, idx = lax.top_k(bscore, K)
        valid = jnp.arange(K) < blk
        kg = kb[bi, hi, idx]
        vg = vb[bi, hi, idx]
        s_sel = jnp.einsum('bhqd,bhqkld->bhqkl', qc, kg).astype(jnp.float32) * scale
        s_sel = jnp.where(valid[:, None], s_sel, -jnp.inf)
        k_own = lax.dynamic_index_in_dim(kb, blk, axis=2, keepdims=False)
        v_own = lax.dynamic_index_in_dim(vb, blk, axis=2, keepdims=False)
        s_own = jnp.einsum('bhqd,bhld->bhql', qc, k_own).astype(jnp.float32) * scale
        qpos = q0 + jnp.arange(MOBA_QCHUNK)
        kpos = blk * MOBA_BLOCK + jnp.arange(MOBA_BLOCK)
        s_own = jnp.where(kpos[None, :] <= qpos[:, None], s_own, -jnp.inf)
        logits = jnp.concatenate([s_sel.reshape(B, H, MOBA_QCHUNK, K * MOBA_BLOCK), s_own], axis=-1)
        p = jax.nn.softmax(logits, axis=-1).astype(v.dtype)
        p_sel = p[..., :K * MOBA_BLOCK].reshape(B, H, MOBA_QCHUNK, K, MOBA_BLOCK)
        p_own = p[..., K * MOBA_BLOCK:]
        return (jnp.einsum('bhqkl,bhqkld->bhqd', p_sel, vg)
                + jnp.einsum('bhql,bhld->bhqd', p_own, v_own))

    out = lax.map(one_chunk, jnp.arange(nq))
    out = jnp.moveaxis(out, 0, 2).reshape(B, H, Sp, hd)
    return out[:, :, :S]


def peer_ffn(xn, w_query, subkeys, u_tab, v_tab):
    B, S, D = xn.shape
    T = B * S
    xt = xn.reshape(T, D)
    q = (xt @ w_query).reshape(T, PEER_HEADS, 2, PEER_QDIM // 2)
    s = jnp.einsum('thpd,hpkd->thpk', q, subkeys).astype(jnp.float32)
    sv, si = lax.top_k(s, PEER_TOPK)
    cand = (sv[:, :, 0, :, None] + sv[:, :, 1, None, :]).reshape(T, PEER_HEADS, PEER_TOPK * PEER_TOPK)
    cand_id = (si[:, :, 0, :, None] * PEER_NKEYS + si[:, :, 1, None, :]).reshape(T, PEER_HEADS, PEER_TOPK * PEER_TOPK)
    top_s, pos = lax.top_k(cand, PEER_TOPK)
    expert = jnp.take_along_axis(cand_id, pos, axis=-1)
    gate = jax.nn.softmax(top_s, axis=-1)
    nchunk = T // PEER_TOKCHUNK

    def apply(args):
        xc, ec, gc = args
        ug = u_tab[ec]
        vg = v_tab[ec]
        act = jax.nn.gelu(jnp.einsum('thkd,td->thk', ug, xc).astype(jnp.float32), approximate=False)
        w = (gc * act).astype(vg.dtype)
        return jnp.einsum('thk,thkd->td', w, vg)

    out = lax.map(apply, (xt.reshape(nchunk, PEER_TOKCHUNK, D),
                          expert.reshape(nchunk, PEER_TOKCHUNK, PEER_HEADS, PEER_TOPK),
                          gate.reshape(nchunk, PEER_TOKCHUNK, PEER_HEADS, PEER_TOPK)))
    return out.reshape(B, S, D)


def setup_inputs(seed: int = 0) -> dict:
    key = jax.random.key(seed)
    ks = jax.random.split(key, 15)
    L, D = DEPTH, D_MODEL
    nrm = jax.random.normal
    return {
        "x": nrm(ks[0], (BATCH, SEQ, D), jnp.float32),
        "norm1_w": 1.0 + 0.02 * nrm(ks[1], (L, D), jnp.float32),
        "w_in": nrm(ks[2], (L, D, IN_WIDTH), jnp.float32) * D ** -0.5,
        "gla_w_alpha": nrm(ks[3], (L, GLA_RANK, GLA_HEADS * GLA_DK), jnp.float32) * GLA_RANK ** -0.5,
        "gla_b_alpha": 0.1 * nrm(ks[4], (L, GLA_HEADS * GLA_DK), jnp.float32),
        "gla_out_norm_w": 1.0 + 0.02 * nrm(ks[5], (L, GLA_DV), jnp.float32),
        "moba_q_norm_w": 1.0 + 0.02 * nrm(ks[6], (L, MOBA_HD), jnp.float32),
        "moba_k_norm_w": 1.0 + 0.02 * nrm(ks[7], (L, MOBA_HD), jnp.float32),
        "mix_scale": 1.0 + 0.02 * nrm(ks[8], (L, MIX_WIDTH), jnp.float32),
        "w_out": nrm(ks[9], (L, MIX_WIDTH, D), jnp.float32) * MIX_WIDTH ** -0.5,
        "norm2_w": 1.0 + 0.02 * nrm(ks[10], (L, D), jnp.float32),
        "peer_w_query": nrm(ks[11], (L, D, PEER_HEADS * PEER_QDIM), jnp.float32) * D ** -0.5,
        "peer_subkeys": nrm(ks[12], (L, PEER_HEADS, 2, PEER_NKEYS, PEER_QDIM // 2), jnp.float32) * (PEER_QDIM // 2) ** -0.5,
        "peer_u": nrm(ks[13], (L, PEER_N, D), jnp.float32) * D ** -0.5,
        "peer_v": nrm(ks[14], (L, PEER_N, D), jnp.float32) * (PEER_HEADS * PEER_TOPK) ** -0.5,
    }


def reference(x, norm1_w, w_in, gla_w_alpha, gla_b_alpha, gla_out_norm_w, moba_q_norm_w,
              moba_k_norm_w, mix_scale, w_out, norm2_w, peer_w_query, peer_subkeys, peer_u, peer_v):
    B, S, _ = x.shape
    pos = jnp.arange(S)
    offsets = []
    acc = 0
    for w in IN_SPLITS[:-1]:
        acc += w
        offsets.append(acc)
    for l in range(DEPTH):
        xn = rmsnorm(x, norm1_w[l])
        proj = xn @ w_in[l]
        gq, gk, gv, ggate, gr, mq, mk, mv = jnp.split(proj, offsets, axis=-1)
        log_a = jax.nn.log_sigmoid((gr @ gla_w_alpha[l] + gla_b_alpha[l]).astype(jnp.float32)) / GLA_TAU
        o_gla = gla_chunked(to_heads(gq, GLA_HEADS).astype(jnp.float32),
                            to_heads(gk, GLA_HEADS).astype(jnp.float32),
                            to_heads(gv, GLA_HEADS).astype(jnp.float32),
                            to_heads(log_a, GLA_HEADS))
        o_gla = from_heads(rmsnorm(o_gla, gla_out_norm_w[l])).astype(x.dtype)
        o_gla = o_gla * jax.nn.silu(ggate)
        qh = rope(rmsnorm(to_heads(mq, MOBA_HEADS), moba_q_norm_w[l]), pos)
        kh = rope(rmsnorm(to_heads(mk, MOBA_HEADS), moba_k_norm_w[l]), pos)
        o_moba = from_heads(moba_attention(qh, kh, to_heads(mv, MOBA_HEADS)))
        mixed = jnp.concatenate([o_gla, o_moba], axis=-1) * mix_scale[l]
        x = x + mixed @ w_out[l]
        x = x + peer_ffn(rmsnorm(x, norm2_w[l]), peer_w_query[l], peer_subkeys[l], peer_u[l], peer_v[l])
    return x
```

```python
import functools
import math

import jax
import jax.numpy as jnp
from jax import lax
from jax.experimental import pallas as pl
from jax.experimental.pallas import tpu as pltpu

D_MODEL = 1024
GLA_HEADS, GLA_DK, GLA_DV, GLA_RANK, GLA_TAU, GLA_CHUNK = 4, 64, 128, 16, 16.0, 64
MOBA_HEADS, MOBA_HD, MOBA_BLOCK, MOBA_TOPK = 8, 64, 256, 3
ROPE_THETA = 10000.0
PEER_HEADS, PEER_NKEYS, PEER_QDIM, PEER_TOPK = 8, 128, 256, 16
PEER_N = PEER_NKEYS * PEER_NKEYS
EPS = 1e-6

GLA_QK = GLA_HEADS * GLA_DK
GLA_V = GLA_HEADS * GLA_DV
MOBA_W = MOBA_HEADS * MOBA_HD
PROJ_W = 2 * GLA_QK + 2 * GLA_V + 3 * MOBA_W + 128
GR_COLBLOCK = (PROJ_W - 128) // 128

F32 = jnp.float32
BF16 = jnp.bfloat16
HIGHEST = lax.Precision.HIGHEST
NEG = -1e30

VMEM_LIMIT_BYTES = 56 * 1024 * 1024

IN_TILE = 512
GLA_TILE = 256
OUT_TILE = 256
ROUTE_TILE = 128
PEER_TILE = 512
PEER_ECHUNK = 1024


def _params(*sem):
    return pltpu.CompilerParams(dimension_semantics=sem, vmem_limit_bytes=VMEM_LIMIT_BYTES)


def _nt_dot(a, b, precision=None):
    return lax.dot_general(a, b, (((1,), (1,)), ((), ())), precision=precision,
                           preferred_element_type=F32)


def _inproj_kernel(x_ref, nw_ref, w_ref, o_ref):
    x = x_ref[...]
    ms = jnp.mean(x * x, axis=-1, keepdims=True)
    xn = (x * lax.rsqrt(ms + EPS) * nw_ref[...]).astype(BF16)
    o_ref[...] = jnp.dot(xn, w_ref[...], preferred_element_type=F32)


def _inproj(x2, norm_w, w):
    T = x2.shape[0]
    return pl.pallas_call(
        _inproj_kernel,
        grid=(T // IN_TILE,),
        in_specs=[pl.BlockSpec((IN_TILE, D_MODEL), lambda i: (i, 0)),
                  pl.BlockSpec((1, D_MODEL), lambda i: (0, 0)),
                  pl.BlockSpec((D_MODEL, PROJ_W), lambda i: (0, 0))],
        out_specs=pl.BlockSpec((IN_TILE, PROJ_W), lambda i: (i, 0)),
        out_shape=jax.ShapeDtypeStruct((T, PROJ_W), F32),
        compiler_params=_params("arbitrary"),
        name="inproj",
    )(x2, norm_w, w)


def _gla_kernel(qk_ref, v_ref, gate_ref, gr_ref, wa_ref, ba_ref, nw_ref, ms_ref, o_ref, st_ref):
    @pl.when(pl.program_id(1) == 0)
    def _():
        st_ref[...] = jnp.zeros_like(st_ref)

    C = GLA_CHUNK
    z = jnp.dot(gr_ref[...], wa_ref[...], precision=HIGHEST, preferred_element_type=F32) + ba_ref[...]
    log_a = (jnp.minimum(z, 0.0) - jnp.log1p(jnp.exp(-jnp.abs(z)))) * (1.0 / GLA_TAU)
    row = lax.broadcasted_iota(jnp.int32, (C, C), 0)
    col = lax.broadcasted_iota(jnp.int32, (C, C), 1)
    causal = col <= row
    tril = causal.astype(F32)
    for c in range(GLA_TILE // C):
        sl = slice(c * C, (c + 1) * C)
        b = jnp.dot(tril, log_a[sl], precision=HIGHEST, preferred_element_type=F32)
        b_last = b[C - 1:C, :]
        q = qk_ref[sl, 0:GLA_QK]
        k = qk_ref[sl, GLA_QK:2 * GLA_QK]
        q_d = q * jnp.exp(b) * (GLA_DK ** -0.5)
        k_in = k * jnp.exp(-b)
        k_st = (k * jnp.exp(b_last - b)).astype(BF16)
        decay = jnp.exp(b_last)
        for h in range(GLA_HEADS):
            ks = slice(h * GLA_DK, (h + 1) * GLA_DK)
            vs = slice(h * GLA_DV, (h + 1) * GLA_DV)
            v_h = v_ref[sl, vs]
            v_bf = v_h.astype(BF16)
            st = st_ref[h]
            a = jnp.where(causal, _nt_dot(q_d[:, ks], k_in[:, ks], precision=HIGHEST), 0.0)
            o = jnp.dot(a.astype(BF16), v_bf, preferred_element_type=F32)
            o = o + _nt_dot(q_d[:, ks].astype(BF16), st.astype(BF16))
            upd = jnp.dot(v_h.T.astype(BF16), k_st[:, ks], preferred_element_type=F32)
            st_ref[h] = st * decay[:, ks] + upd
            o = o * lax.rsqrt(jnp.mean(o * o, axis=-1, keepdims=True) + EPS) * nw_ref[...]
            g = gate_ref[sl, vs]
            o = o * (g * jax.nn.sigmoid(g)) * ms_ref[:, vs]
            o_ref[sl, vs] = o.astype(BF16)


def _gla(proj, B, S, w_alpha, b_alpha, out_norm_w, mix_scale_gla):
    nt = S // GLA_TILE
    tok = lambda b, j: b * nt + j
    return pl.pallas_call(
        _gla_kernel,
        grid=(B, nt),
        in_specs=[pl.BlockSpec((GLA_TILE, 2 * GLA_QK), lambda b, j: (tok(b, j), 0)),
                  pl.BlockSpec((GLA_TILE, GLA_V), lambda b, j: (tok(b, j), 1)),
                  pl.BlockSpec((GLA_TILE, GLA_V), lambda b, j: (tok(b, j), 2)),
                  pl.BlockSpec((GLA_TILE, 128), lambda b, j: (tok(b, j), GR_COLBLOCK)),
                  pl.BlockSpec((128, GLA_QK), lambda b, j: (0, 0)),
                  pl.BlockSpec((1, GLA_QK), lambda b, j: (0, 0)),
                  pl.BlockSpec((1, GLA_DV), lambda b, j: (0, 0)),
                  pl.BlockSpec((1, GLA_V), lambda b, j: (0, 0))],
        out_specs=pl.BlockSpec((GLA_TILE, GLA_V), lambda b, j: (tok(b, j), 0)),
        out_shape=jax.ShapeDtypeStruct((B * S, GLA_V), BF16),
        scratch_shapes=[pltpu.VMEM((GLA_HEADS, GLA_DV, GLA_DK), F32)],
        compiler_params=_params("arbitrary", "arbitrary"),
        name="gla",
    )(proj, proj, proj, proj, w_alpha, b_alpha, out_norm_w, mix_scale_gla)


def _norm_rope_t(xt, w_ref, cos_ref, sin_ref):
    half = MOBA_HD // 2
    outs = []
    for h in range(MOBA_HEADS):
        xh = xt[h * MOBA_HD:(h + 1) * MOBA_HD]
        ms = jnp.mean(xh * xh, axis=0, keepdims=True)
        y = xh * lax.rsqrt(ms + EPS) * w_ref[...]
        y1, y2 = y[:half], y[half:]
        c, s = cos_ref[...], sin_ref[...]
        outs.append(y1 * c - y2 * s)
        outs.append(y2 * c + y1 * s)
    return jnp.concatenate(outs, axis=0)


def _moba_prep_kernel(q_ref, k_ref, v_ref, qw_ref, kw_ref, cos_ref, sin_ref,
                      qt_ref, kh_ref, vt_ref, bias_ref, mt_ref):
    j = pl.program_id(1)
    nb = 16

    @pl.when(j == 0)
    def _():
        mt_ref[...] = jnp.zeros_like(mt_ref)

    qt = _norm_rope_t(q_ref[...].T, qw_ref, cos_ref, sin_ref)
    kt = _norm_rope_t(k_ref[...].T, kw_ref, cos_ref, sin_ref)
    qt_ref[0] = (qt * (MOBA_HD ** -0.5)).astype(BF16)
    kn = kt.T
    for h in range(MOBA_HEADS):
        kh_ref[0, h] = kn[:, h * MOBA_HD:(h + 1) * MOBA_HD].astype(BF16)
    vt_ref[0, 0] = v_ref[...].T.astype(BF16)

    bs = jnp.dot(mt_ref[...], qt, precision=HIGHEST, preferred_element_type=F32)
    n_idx = lax.broadcasted_iota(jnp.int32, (nb, MOBA_BLOCK), 0)
    for h in range(MOBA_HEADS):
        sc = jnp.where(n_idx < j, bs[h * nb:(h + 1) * nb], -jnp.inf)
        sel = jnp.zeros((nb, MOBA_BLOCK), jnp.bool_)
        for r in range(MOBA_TOPK):
            m = jnp.max(sc, axis=0, keepdims=True)
            first = jnp.min(jnp.where(sc == m, n_idx, nb), axis=0, keepdims=True)
            hit = n_idx == first
            sel = jnp.logical_or(sel, jnp.logical_and(hit, r < j))
            sc = jnp.where(hit, -jnp.inf, sc)
        bias_ref[0, 0, h] = jnp.where(sel, 0.0, NEG)

    kbar = jnp.mean(kn, axis=0, keepdims=True)
    r_idx = lax.broadcasted_iota(jnp.int32, (MOBA_HEADS * nb, MOBA_W), 0)
    c_idx = lax.broadcasted_iota(jnp.int32, (MOBA_HEADS * nb, MOBA_W), 1)
    mine = jnp.logical_and(r_idx % nb == j, r_idx // nb == c_idx // MOBA_HD)
    mt_ref[...] = jnp.where(mine, kbar, mt_ref[...])


def _moba_prep(proj, B, S, qw, kw, cos_t, sin_t):
    nb = S // MOBA_BLOCK
    assert nb <= 16
    tok = lambda b, j: b * nb + j
    half = MOBA_HD // 2
    return pl.pallas_call(
        _moba_prep_kernel,
        grid=(B, nb),
        in_specs=[pl.BlockSpec((MOBA_BLOCK, MOBA_W), lambda b, j: (tok(b, j), 3)),
                  pl.BlockSpec((MOBA_BLOCK, MOBA_W), lambda b, j: (tok(b, j), 4)),
                  pl.BlockSpec((MOBA_BLOCK, MOBA_W), lambda b, j: (tok(b, j), 5)),
                  pl.BlockSpec((MOBA_HD, MOBA_BLOCK), lambda b, j: (0, 0)),
                  pl.BlockSpec((MOBA_HD, MOBA_BLOCK), lambda b, j: (0, 0)),
                  pl.BlockSpec((half, MOBA_BLOCK), lambda b, j: (0, j)),
                  pl.BlockSpec((half, MOBA_BLOCK), lambda b, j: (0, j))],
        out_specs=[pl.BlockSpec((1, MOBA_W, MOBA_BLOCK), lambda b, j: (b, 0, j)),
                   pl.BlockSpec((1, MOBA_HEADS, MOBA_BLOCK, MOBA_HD), lambda b, j: (b, 0, j, 0)),
                   pl.BlockSpec((1, 1, MOBA_W, MOBA_BLOCK), lambda b, j: (b, j, 0, 0)),
                   pl.BlockSpec((1, 1, MOBA_HEADS, 16, MOBA_BLOCK), lambda b, j: (b, j, 0, 0, 0))],
        out_shape=[jax.ShapeDtypeStruct((B, MOBA_W, S), BF16),
                   jax.ShapeDtypeStruct((B, MOBA_HEADS, S, MOBA_HD), BF16),
                   jax.ShapeDtypeStruct((B, nb, MOBA_W, MOBA_BLOCK), BF16),
                   jax.ShapeDtypeStruct((B, nb, MOBA_HEADS, 16, MOBA_BLOCK), F32)],
        scratch_shapes=[pltpu.VMEM((MOBA_HEADS * 16, MOBA_W), F32)],
        compiler_params=_params("arbitrary", "arbitrary"),
        name="moba_prep",
    )(proj, proj, proj, qw, kw, cos_t, sin_t)


def _moba_kernel(qt_ref, k_ref, vt_ref, bias_ref, o_ref):
    j = pl.program_id(2)
    qt = qt_ref[0]
    k_own = k_ref[0, 0, pl.ds(pl.multiple_of(j * MOBA_BLOCK, MOBA_BLOCK), MOBA_BLOCK), :]
    s = jnp.dot(k_own, qt, preferred_element_type=F32)
    kpos = lax.broadcasted_iota(jnp.int32, s.shape, 0)
    qpos = lax.broadcasted_iota(jnp.int32, s.shape, 1)
    s = jnp.where(kpos <= qpos, s, -jnp.inf)
    m = jnp.max(s, axis=0, keepdims=True)
    p = jnp.exp(s - m)
    l = jnp.sum(p, axis=0, keepdims=True)
    acc = jnp.dot(vt_ref[0, j], p.astype(BF16), preferred_element_type=F32)

    def body(n, carry):
        m, l, acc = carry
        k_n = k_ref[0, 0, pl.ds(pl.multiple_of(n * MOBA_BLOCK, MOBA_BLOCK), MOBA_BLOCK), :]
        s = jnp.dot(k_n, qt, preferred_element_type=F32) + bias_ref[0, 0, 0, pl.ds(n, 1), :]
        m_new = jnp.maximum(m, jnp.max(s, axis=0, keepdims=True))
        alpha = jnp.exp(m - m_new)
        p = jnp.exp(s - m_new)
        l = alpha * l + jnp.sum(p, axis=0, keepdims=True)
        acc = acc * alpha + jnp.dot(vt_ref[0, n], p.astype(BF16), preferred_element_type=F32)
        return m_new, l, acc

    m, l, acc = lax.fori_loop(0, j, body, (m, l, acc))
    o_ref[0] = (acc / l).astype(BF16)


def _moba(qt, kh, vt, bias, B, S):
    nb = S // MOBA_BLOCK
    return pl.pallas_call(
        _moba_kernel,
        grid=(B, MOBA_HEADS, nb),
        in_specs=[pl.BlockSpec((1, MOBA_HD, MOBA_BLOCK), lambda b, h, j: (b, h, j)),
                  pl.BlockSpec((1, 1, S, MOBA_HD), lambda b, h, j: (b, h, 0, 0)),
                  pl.BlockSpec((1, nb, MOBA_HD, MOBA_BLOCK), lambda b, h, j: (b, 0, h, 0)),
                  pl.BlockSpec((1, 1, 1, 16, MOBA_BLOCK), lambda b, h, j: (b, j, h, 0, 0))],
        out_specs=pl.BlockSpec((1, MOBA_HD, MOBA_BLOCK), lambda b, h, j: (b, h, j)),
        out_shape=jax.ShapeDtypeStruct((B, MOBA_W, S), BF16),
        compiler_params=_params("arbitrary", "arbitrary", "arbitrary"),
        name="moba",
    )(qt, kh, vt, bias)


def _outproj_kernel(x_ref, og_ref, omt_ref, msm_ref, wo_ref, n2_ref, wq_ref, x1_ref, xn_ref, q_ref):
    om = (omt_ref[0].astype(F32).T * msm_ref[...]).astype(BF16)
    x1 = (x_ref[...]
          + jnp.dot(og_ref[...], wo_ref[0:GLA_V, :], preferred_element_type=F32)
          + jnp.dot(om, wo_ref[GLA_V:, :], preferred_element_type=F32))
    x1_ref[...] = x1
    ms = jnp.mean(x1 * x1, axis=-1, keepdims=True)
    xn = (x1 * lax.rsqrt(ms + EPS) * n2_ref[...]).astype(BF16)
    xn_ref[...] = xn
    q_ref[...] = jnp.dot(xn, wq_ref[...], preferred_element_type=F32).astype(BF16)


def _outproj(x2, o_gla, o_moba_t, ms_moba, w_out, norm2_w, wq, B, S):
    T = B * S
    nt = S // OUT_TILE
    QW = PEER_HEADS * PEER_QDIM
    return pl.pallas_call(
        _outproj_kernel,
        grid=(T // OUT_TILE,),
        in_specs=[pl.BlockSpec((OUT_TILE, D_MODEL), lambda i: (i, 0)),
                  pl.BlockSpec((OUT_TILE, GLA_V), lambda i: (i, 0)),
                  pl.BlockSpec((1, MOBA_W, OUT_TILE), lambda i: (i // nt, 0, i % nt)),
                  pl.BlockSpec((1, MOBA_W), lambda i: (0, 0)),
                  pl.BlockSpec((GLA_V + MOBA_W, D_MODEL), lambda i: (0, 0)),
                  pl.BlockSpec((1, D_MODEL), lambda i: (0, 0)),
                  pl.BlockSpec((D_MODEL, QW), lambda i: (0, 0))],
        out_specs=[pl.BlockSpec((OUT_TILE, D_MODEL), lambda i: (i, 0)),
                   pl.BlockSpec((OUT_TILE, D_MODEL), lambda i: (i, 0)),
                   pl.BlockSpec((OUT_TILE, QW), lambda i: (i, 0))],
        out_shape=[jax.ShapeDtypeStruct((T, D_MODEL), F32),
                   jax.ShapeDtypeStruct((T, D_MODEL), BF16),
                   jax.ShapeDtypeStruct((T, QW), BF16)],
        compiler_params=_params("arbitrary"),
        name="outproj",
    )(x2, o_gla, o_moba_t, ms_moba, w_out, norm2_w, wq)


_PAIRS = [(r, q) for r in range(PEER_TOPK + 1) for q in range(PEER_TOPK + 1)
          if (r + 1) * (q + 1) <= PEER_TOPK + 1]


def _route_kernel(q_ref, sk_ref, s1_ref, e1_ref, th_ref, e0_ref, sc_ref):
    H, NK = PEER_HEADS, PEER_NKEYS
    hq = H * PEER_QDIM // 2
    tile = q_ref.shape[0]
    for p in range(2):
        sc_ref[p] = _nt_dot(sk_ref[p], q_ref[:, p * hq:(p + 1) * hq])

    def next_largest(p, below):
        def body(kk, m):
            v = sc_ref[p, pl.ds(pl.multiple_of(kk * H, H), H), :]
            return jnp.maximum(m, jnp.where(v < below, v, -jnp.inf))
        return lax.fori_loop(0, NK, body, jnp.full((H, tile), -jnp.inf, F32), unroll=8)

    tops = []
    for p in range(2):
        vals, below = [], jnp.full((H, tile), jnp.inf, F32)
        for _ in range(PEER_TOPK + 1):
            below = next_largest(p, below)
            vals.append(below)
        tops.append(vals)
    cands = [tops[0][r] + tops[1][q] for r, q in _PAIRS]
    below = jnp.full((H, tile), jnp.inf, F32)
    for r in range(PEER_TOPK + 1):
        m = jnp.full((H, tile), -jnp.inf, F32)
        for c in cands:
            m = jnp.maximum(m, jnp.where(c < below, c, -jnp.inf))
        if r == PEER_TOPK - 1:
            c16 = m
        below = m
    tau = 0.5 * (c16 + below)
    top = tops[0][0] + tops[1][0]
    zsum = jnp.zeros((H, tile), F32)
    for c in cands:
        zsum = zsum + jnp.where(c > tau, jnp.exp(c - top), 0.0)
    zinv = 1.0 / zsum
    for h in range(H):
        s0 = sc_ref[0, pl.ds(h, NK, stride=H), :]
        s1 = sc_ref[1, pl.ds(h, NK, stride=H), :]
        th_ref[h] = tau[h:h + 1] - s0
        e0_ref[h] = jnp.exp(s0 - tops[0][0][h:h + 1]) * zinv[h:h + 1]
        s1_ref[h] = s1
        e1_ref[h] = jnp.exp(s1 - tops[1][0][h:h + 1])


def _route(q, skbig):
    T = q.shape[0]
    H, NK = PEER_HEADS, PEER_NKEYS
    hq = H * PEER_QDIM // 2
    spec = pl.BlockSpec((H, NK, ROUTE_TILE), lambda i: (0, 0, i))
    shp = jax.ShapeDtypeStruct((H, NK, T), F32)
    return pl.pallas_call(
        _route_kernel,
        grid=(T // ROUTE_TILE,),
        in_specs=[pl.BlockSpec((ROUTE_TILE, 2 * hq), lambda i: (i, 0)),
                  pl.BlockSpec((2, NK * H, hq), lambda i: (0, 0, 0))],
        out_specs=[spec, spec, spec, spec],
        out_shape=[shp, shp, shp, shp],
        scratch_shapes=[pltpu.VMEM((2, NK * H, ROUTE_TILE), F32)],
        compiler_params=_params("arbitrary"),
        name="peer_route",
    )(q, skbig)


def _peer_kernel(xn_ref, x1_ref, u_ref, vt_ref, s1_ref, e1_ref, th_ref, e0_ref, o_ref, acc_ref, w_ref):
    c = pl.program_id(1)
    NK = PEER_NKEYS
    nsub = PEER_ECHUNK // NK

    @pl.when(c == 0)
    def _():
        acc_ref[...] = jnp.zeros_like(acc_ref)

    act = _nt_dot(u_ref[...], xn_ref[...])
    for ii in range(nsub):
        i = c * nsub + ii
        a = act[ii * NK:(ii + 1) * NK]
        g = 0.5 * a * (1.0 + lax.erf(a * (1.0 / math.sqrt(2.0))))
        wsum = jnp.zeros_like(a)
        for h in range(PEER_HEADS):
            th = th_ref[h, pl.ds(i, 1), :]
            e0 = e0_ref[h, pl.ds(i, 1), :]
            wsum = wsum + e0 * jnp.where(s1_ref[h] >= th, e1_ref[h], 0.0)
        w_ref[ii * NK:(ii + 1) * NK, :] = (g * wsum).astype(BF16)
    acc_ref[...] += jnp.dot(vt_ref[...], w_ref[...], preferred_element_type=F32)

    @pl.when(c == pl.num_programs(1) - 1)
    def _():
        o_ref[...] = x1_ref[...] + acc_ref[...].T


def _peer(xn, x1, u, vt, s1, e1, th, e0):
    T = xn.shape[0]
    H, NK = PEER_HEADS, PEER_NKEYS
    rspec = pl.BlockSpec((H, NK, PEER_TILE), lambda t, c: (0, 0, t))
    return pl.pallas_call(
        _peer_kernel,
        grid=(T // PEER_TILE, PEER_N // PEER_ECHUNK),
        in_specs=[pl.BlockSpec((PEER_TILE, D_MODEL), lambda t, c: (t, 0)),
                  pl.BlockSpec((PEER_TILE, D_MODEL), lambda t, c: (t, 0)),
                  pl.BlockSpec((PEER_ECHUNK, D_MODEL), lambda t, c: (c, 0)),
                  pl.BlockSpec((D_MODEL, PEER_ECHUNK), lambda t, c: (0, c)),
                  rspec, rspec, rspec, rspec],
        out_specs=pl.BlockSpec((PEER_TILE, D_MODEL), lambda t, c: (t, 0)),
        out_shape=jax.ShapeDtypeStruct((T, D_MODEL), F32),
        scratch_shapes=[pltpu.VMEM((D_MODEL, PEER_TILE), F32),
                        pltpu.VMEM((PEER_ECHUNK, PEER_TILE), BF16)],
        compiler_params=_params("arbitrary", "arbitrary"),
        name="peer",
    )(xn, x1, u, vt, s1, e1, th, e0)


def _layer(x, norm1_w, w_in, w_alpha, b_alpha, out_norm_w, qn_w, kn_w, mix_scale, w_out,
           norm2_w, w_query, subkeys, u_tab, v_tab):
    B, S, D = x.shape
    T = B * S
    x2 = x.reshape(T, D)

    o_gr = 2 * GLA_QK + 2 * GLA_V
    w_r = jnp.concatenate([w_in[:, :o_gr], w_in[:, o_gr + GLA_RANK:], w_in[:, o_gr:o_gr + GLA_RANK],
                           jnp.zeros((D, 128 - GLA_RANK), w_in.dtype)], axis=1).astype(BF16)
    wa = jnp.concatenate([w_alpha, jnp.zeros((128 - GLA_RANK, GLA_QK), w_alpha.dtype)], axis=0)

    proj = _inproj(x2, norm1_w.reshape(1, D), w_r)
    o_gla = _gla(proj, B, S, wa, b_alpha.reshape(1, GLA_QK), out_norm_w.reshape(1, GLA_DV),
                 mix_scale[:GLA_V].reshape(1, GLA_V))

    half = MOBA_HD // 2
    inv = ROPE_THETA ** (-jnp.arange(half, dtype=F32) / half)
    ang = inv[:, None] * jnp.arange(S, dtype=F32)[None, :]
    qw = jnp.broadcast_to(qn_w.reshape(MOBA_HD, 1), (MOBA_HD, MOBA_BLOCK))
    kw = jnp.broadcast_to(kn_w.reshape(MOBA_HD, 1), (MOBA_HD, MOBA_BLOCK))
    qt, kh, vt, bias = _moba_prep(proj, B, S, qw, kw, jnp.cos(ang), jnp.sin(ang))
    o_moba_t = _moba(qt, kh, vt, bias, B, S)

    H, NK, hd = PEER_HEADS, PEER_NKEYS, PEER_QDIM // 2
    wq = w_query.reshape(D, H, 2, hd).transpose(0, 2, 1, 3).reshape(D, 2 * H * hd).astype(BF16)
    eye = jnp.eye(H, dtype=subkeys.dtype)
    skbig = jnp.einsum('hpkd,hg->pkhgd', subkeys, eye).reshape(2, NK * H, H * hd).astype(BF16)

    x1, xn, q = _outproj(x2, o_gla, o_moba_t, mix_scale[GLA_V:].reshape(1, MOBA_W),
                         w_out.astype(BF16), norm2_w.reshape(1, D), wq, B, S)
    s1, e1, th, e0 = _route(q, skbig)
    out = _peer(xn, x1, u_tab.astype(BF16), v_tab.T.astype(BF16), s1, e1, th, e0)
    return out.reshape(B, S, D)


def kernel(x, norm1_w, w_in, gla_w_alpha, gla_b_alpha, gla_out_norm_w, moba_q_norm_w, moba_k_norm_w,
           mix_scale, w_out, norm2_w, peer_w_query, peer_subkeys, peer_u, peer_v):
    assert norm1_w.shape[0] == 1, "single-layer kernel"
    return _layer(x, norm1_w[0], w_in[0], gla_w_alpha[0], gla_b_alpha[0], gla_out_norm_w[0],
                  moba_q_norm_w[0], moba_k_norm_w[0], mix_scale[0], w_out[0], norm2_w[0],
                  peer_w_query[0], peer_subkeys[0], peer_u[0], peer_v[0])
```

```python
import functools
import math

import jax
import jax.numpy as jnp
from jax import lax
from jax.experimental import pallas as pl
from jax.experimental.pallas import tpu as pltpu

D_MODEL = 1024
GLA_HEADS, GLA_DK, GLA_DV, GLA_RANK, GLA_TAU, GLA_CHUNK = 4, 64, 128, 16, 16.0, 64
MOBA_HEADS, MOBA_HD, MOBA_BLOCK, MOBA_TOPK = 8, 64, 256, 3
ROPE_THETA = 10000.0
PEER_HEADS, PEER_NKEYS, PEER_QDIM, PEER_TOPK = 8, 128, 256, 16
PEER_N = PEER_NKEYS * PEER_NKEYS
EPS = 1e-6

GLA_QK = GLA_HEADS * GLA_DK
GLA_V = GLA_HEADS * GLA_DV
MOBA_W = MOBA_HEADS * MOBA_HD
PROJ_W = 2 * GLA_QK + 2 * GLA_V + 3 * MOBA_W + 128
GR_COLBLOCK = (PROJ_W - 128) // 128

F32 = jnp.float32
BF16 = jnp.bfloat16
HIGHEST = lax.Precision.HIGHEST
NEG = -1e30

VMEM_LIMIT_BYTES = 56 * 1024 * 1024

IN_TILE = 512
GLA_TILE = 256
OUT_TILE = 256
ROUTE_TILE = 128
PEER_TILE = 512
PEER_ECHUNK = 1024
PEER_STRIP_ROWS = 32
PEER_MXU_BLOCK = 256
LANES = 256


def _params(*sem):
    return pltpu.CompilerParams(dimension_semantics=sem, vmem_limit_bytes=VMEM_LIMIT_BYTES)


def _nt_dot(a, b, precision=None):
    return lax.dot_general(a, b, (((1,), (1,)), ((), ())), precision=precision,
                           preferred_element_type=F32)


def _inproj_kernel(x_ref, nw_ref, w_ref, o_ref):
    x = x_ref[...]
    ms = jnp.mean(x * x, axis=-1, keepdims=True)
    xn = (x * lax.rsqrt(ms + EPS) * nw_ref[...]).astype(BF16)
    o_ref[...] = jnp.dot(xn, w_ref[...], preferred_element_type=F32)


def _inproj(x2, norm_w, w):
    T = x2.shape[0]
    return pl.pallas_call(
        _inproj_kernel,
        grid=(T // IN_TILE,),
        in_specs=[pl.BlockSpec((IN_TILE, D_MODEL), lambda i: (i, 0)),
                  pl.BlockSpec((1, D_MODEL), lambda i: (0, 0)),
                  pl.BlockSpec((D_MODEL, PROJ_W), lambda i: (0, 0))],
        out_specs=pl.BlockSpec((IN_TILE, PROJ_W), lambda i: (i, 0)),
        out_shape=jax.ShapeDtypeStruct((T, PROJ_W), F32),
        compiler_params=_params("arbitrary"),
        name="inproj",
    )(x2, norm_w, w)


def _gla_kernel(qk_ref, v_ref, gate_ref, gr_ref, wa_ref, ba_ref, nw_ref, ms_ref, o_ref, st_ref):
    @pl.when(pl.program_id(1) == 0)
    def _():
        st_ref[...] = jnp.zeros_like(st_ref)

    C = GLA_CHUNK
    z = jnp.dot(gr_ref[...], wa_ref[...], precision=HIGHEST, preferred_element_type=F32) + ba_ref[...]
    log_a = (jnp.minimum(z, 0.0) - jnp.log1p(jnp.exp(-jnp.abs(z)))) * (1.0 / GLA_TAU)
    row = lax.broadcasted_iota(jnp.int32, (C, C), 0)
    col = lax.broadcasted_iota(jnp.int32, (C, C), 1)
    causal = col <= row
    tril = causal.astype(F32)
    for c in range(GLA_TILE // C):
        sl = slice(c * C, (c + 1) * C)
        b = jnp.dot(tril, log_a[sl], precision=HIGHEST, preferred_element_type=F32)
        b_last = b[C - 1:C, :]
        q = qk_ref[sl, 0:GLA_QK]
        k = qk_ref[sl, GLA_QK:2 * GLA_QK]
        q_d = q * jnp.exp(b) * (GLA_DK ** -0.5)
        k_in = k * jnp.exp(-b)
        k_st = (k * jnp.exp(b_last - b)).astype(BF16)
        decay = jnp.exp(b_last)
        for h in range(GLA_HEADS):
            ks = slice(h * GLA_DK, (h + 1) * GLA_DK)
            vs = slice(h * GLA_DV, (h + 1) * GLA_DV)
            v_h = v_ref[sl, vs]
            v_bf = v_h.astype(BF16)
            st = st_ref[h]
            a = jnp.where(causal, _nt_dot(q_d[:, ks], k_in[:, ks], precision=HIGHEST), 0.0)
            o = jnp.dot(a.astype(BF16), v_bf, preferred_element_type=F32)
            o = o + _nt_dot(q_d[:, ks].astype(BF16), st.astype(BF16))
            upd = jnp.dot(v_h.T.astype(BF16), k_st[:, ks], preferred_element_type=F32)
            st_ref[h] = st * decay[:, ks] + upd
            o = o * lax.rsqrt(jnp.mean(o * o, axis=-1, keepdims=True) + EPS) * nw_ref[...]
            g = gate_ref[sl, vs]
            o = o * (g * jax.nn.sigmoid(g)) * ms_ref[:, vs]
            o_ref[sl, vs] = o.astype(BF16)


def _gla(proj, B, S, w_alpha, b_alpha, out_norm_w, mix_scale_gla):
    nt = S // GLA_TILE
    tok = lambda b, j: b * nt + j
    return pl.pallas_call(
        _gla_kernel,
        grid=(B, nt),
        in_specs=[pl.BlockSpec((GLA_TILE, 2 * GLA_QK), lambda b, j: (tok(b, j), 0)),
                  pl.BlockSpec((GLA_TILE, GLA_V), lambda b, j: (tok(b, j), 1)),
                  pl.BlockSpec((GLA_TILE, GLA_V), lambda b, j: (tok(b, j), 2)),
                  pl.BlockSpec((GLA_TILE, 128), lambda b, j: (tok(b, j), GR_COLBLOCK)),
                  pl.BlockSpec((128, GLA_QK), lambda b, j: (0, 0)),
                  pl.BlockSpec((1, GLA_QK), lambda b, j: (0, 0)),
                  pl.BlockSpec((1, GLA_DV), lambda b, j: (0, 0)),
                  pl.BlockSpec((1, GLA_V), lambda b, j: (0, 0))],
        out_specs=pl.BlockSpec((GLA_TILE, GLA_V), lambda b, j: (tok(b, j), 0)),
        out_shape=jax.ShapeDtypeStruct((B * S, GLA_V), BF16),
        scratch_shapes=[pltpu.VMEM((GLA_HEADS, GLA_DV, GLA_DK), F32)],
        compiler_params=_params("arbitrary", "arbitrary"),
        name="gla",
    )(proj, proj, proj, proj, w_alpha, b_alpha, out_norm_w, mix_scale_gla)


def _norm_rope_t(xt, w_ref, cos_ref, sin_ref):
    half = MOBA_HD // 2
    outs = []
    for h in range(MOBA_HEADS):
        xh = xt[h * MOBA_HD:(h + 1) * MOBA_HD]
        ms = jnp.mean(xh * xh, axis=0, keepdims=True)
        y = xh * lax.rsqrt(ms + EPS) * w_ref[...]
        y1, y2 = y[:half], y[half:]
        c, s = cos_ref[...], sin_ref[...]
        outs.append(y1 * c - y2 * s)
        outs.append(y2 * c + y1 * s)
    return jnp.concatenate(outs, axis=0)


def _moba_prep_kernel(q_ref, k_ref, v_ref, qw_ref, kw_ref, cos_ref, sin_ref,
                      qt_ref, kh_ref, vt_ref, bias_ref, mt_ref):
    j = pl.program_id(1)
    nb = 16

    @pl.when(j == 0)
    def _():
        mt_ref[...] = jnp.zeros_like(mt_ref)

    qt = _norm_rope_t(q_ref[...].T, qw_ref, cos_ref, sin_ref)
    kt = _norm_rope_t(k_ref[...].T, kw_ref, cos_ref, sin_ref)
    qt_ref[0] = (qt * (MOBA_HD ** -0.5)).astype(BF16)
    kn = kt.T
    for h in range(MOBA_HEADS):
        kh_ref[0, h] = kn[:, h * MOBA_HD:(h + 1) * MOBA_HD].astype(BF16)
    vt_ref[0, 0] = v_ref[...].T.astype(BF16)

    bs = jnp.dot(mt_ref[...], qt, precision=HIGHEST, preferred_element_type=F32)
    n_idx = lax.broadcasted_iota(jnp.int32, (nb, MOBA_BLOCK), 0)
    for h in range(MOBA_HEADS):
        sc = jnp.where(n_idx < j, bs[h * nb:(h + 1) * nb], -jnp.inf)
        sel = jnp.zeros((nb, MOBA_BLOCK), jnp.bool_)
        for r in range(MOBA_TOPK):
            m = jnp.max(sc, axis=0, keepdims=True)
            first = jnp.min(jnp.where(sc == m, n_idx, nb), axis=0, keepdims=True)
            hit = n_idx == first
            sel = jnp.logical_or(sel, jnp.logical_and(hit, r < j))
            sc = jnp.where(hit, -jnp.inf, sc)
        bias_ref[0, 0, h] = jnp.where(sel, 0.0, NEG)

    kbar = jnp.mean(kn, axis=0, keepdims=True)
    r_idx = lax.broadcasted_iota(jnp.int32, (MOBA_HEADS * nb, MOBA_W), 0)
    c_idx = lax.broadcasted_iota(jnp.int32, (MOBA_HEADS * nb, MOBA_W), 1)
    mine = jnp.logical_and(r_idx % nb == j, r_idx // nb == c_idx // MOBA_HD)
    mt_ref[...] = jnp.where(mine, kbar, mt_ref[...])


def _moba_prep(proj, B, S, qw, kw, cos_t, sin_t):
    nb = S // MOBA_BLOCK
    assert nb <= 16
    tok = lambda b, j: b * nb + j
    half = MOBA_HD // 2
    return pl.pallas_call(
        _moba_prep_kernel,
        grid=(B, nb),
        in_specs=[pl.BlockSpec((MOBA_BLOCK, MOBA_W), lambda b, j: (tok(b, j), 3)),
                  pl.BlockSpec((MOBA_BLOCK, MOBA_W), lambda b, j: (tok(b, j), 4)),
                  pl.BlockSpec((MOBA_BLOCK, MOBA_W), lambda b, j: (tok(b, j), 5)),
                  pl.BlockSpec((MOBA_HD, MOBA_BLOCK), lambda b, j: (0, 0)),
                  pl.BlockSpec((MOBA_HD, MOBA_BLOCK), lambda b, j: (0, 0)),
                  pl.BlockSpec((half, MOBA_BLOCK), lambda b, j: (0, j)),
                  pl.BlockSpec((half, MOBA_BLOCK), lambda b, j: (0, j))],
        out_specs=[pl.BlockSpec((1, MOBA_W, MOBA_BLOCK), lambda b, j: (b, 0, j)),
                   pl.BlockSpec((1, MOBA_HEADS, MOBA_BLOCK, MOBA_HD), lambda b, j: (b, 0, j, 0)),
                   pl.BlockSpec((1, 1, MOBA_W, MOBA_BLOCK), lambda b, j: (b, j, 0, 0)),
                   pl.BlockSpec((1, 1, MOBA_HEADS, 16, MOBA_BLOCK), lambda b, j: (b, j, 0, 0, 0))],
        out_shape=[jax.ShapeDtypeStruct((B, MOBA_W, S), BF16),
                   jax.ShapeDtypeStruct((B, MOBA_HEADS, S, MOBA_HD), BF16),
                   jax.ShapeDtypeStruct((B, nb, MOBA_W, MOBA_BLOCK), BF16),
                   jax.ShapeDtypeStruct((B, nb, MOBA_HEADS, 16, MOBA_BLOCK), F32)],
        scratch_shapes=[pltpu.VMEM((MOBA_HEADS * 16, MOBA_W), F32)],
        compiler_params=_params("arbitrary", "arbitrary"),
        name="moba_prep",
    )(proj, proj, proj, qw, kw, cos_t, sin_t)


def _moba_kernel(qt_ref, k_ref, vt_ref, bias_ref, o_ref, acc_ref, m_ref, l_ref):
    j = pl.program_id(1)
    hd = MOBA_HD
    kpos = lax.broadcasted_iota(jnp.int32, (MOBA_BLOCK, MOBA_BLOCK), 0)
    qpos = lax.broadcasted_iota(jnp.int32, (MOBA_BLOCK, MOBA_BLOCK), 1)
    causal = kpos <= qpos
    own = pl.ds(pl.multiple_of(j * MOBA_BLOCK, MOBA_BLOCK), MOBA_BLOCK)
    heads = range(MOBA_HEADS)
    rows = [slice(h * hd, (h + 1) * hd) for h in heads]
    s = [jnp.where(causal, jnp.dot(k_ref[0, h, own, :], qt_ref[0, rows[h], :], preferred_element_type=F32),
                   -jnp.inf) for h in heads]
    m = [jnp.max(s[h], axis=0, keepdims=True) for h in heads]
    p = [jnp.exp(s[h] - m[h]) for h in heads]
    l = [jnp.sum(p[h], axis=0, keepdims=True) for h in heads]
    pv = [jnp.dot(vt_ref[0, j, rows[h], :], p[h].astype(BF16), preferred_element_type=F32)
          for h in heads]
    for h in heads:
        m_ref[h] = m[h]
        l_ref[h] = l[h]
        acc_ref[h] = pv[h]

    def body(n, carry):
        blk = pl.ds(pl.multiple_of(n * MOBA_BLOCK, MOBA_BLOCK), MOBA_BLOCK)
        s = [jnp.dot(k_ref[0, h, blk, :], qt_ref[0, rows[h], :], preferred_element_type=F32)
             + bias_ref[0, 0, h, pl.ds(n, 1), :] for h in heads]
        m_old = [m_ref[h] for h in heads]
        m_new = [jnp.maximum(m_old[h], jnp.max(s[h], axis=0, keepdims=True)) for h in heads]
        alpha = [jnp.exp(m_old[h] - m_new[h]) for h in heads]
        p = [jnp.exp(s[h] - m_new[h]) for h in heads]
        l_new = [alpha[h] * l_ref[h] + jnp.sum(p[h], axis=0, keepdims=True) for h in heads]
        pv = [jnp.dot(vt_ref[0, n, rows[h], :], p[h].astype(BF16), preferred_element_type=F32)
              for h in heads]
        acc_new = [acc_ref[h] * alpha[h] + pv[h] for h in heads]
        for h in heads:
            m_ref[h] = m_new[h]
            l_ref[h] = l_new[h]
            acc_ref[h] = acc_new[h]
        return carry

    lax.fori_loop(0, j, body, 0)
    for h in range(MOBA_HEADS):
        o_ref[0, h * hd:(h + 1) * hd, :] = (acc_ref[h] / l_ref[h]).astype(BF16)


def _moba(qt, kh, vt, bias, B, S):
    nb = S // MOBA_BLOCK
    return pl.pallas_call(
        _moba_kernel,
        grid=(B, nb),
        in_specs=[pl.BlockSpec((1, MOBA_W, MOBA_BLOCK), lambda b, j: (b, 0, j)),
                  pl.BlockSpec((1, MOBA_HEADS, S, MOBA_HD), lambda b, j: (b, 0, 0, 0)),
                  pl.BlockSpec((1, nb, MOBA_W, MOBA_BLOCK), lambda b, j: (b, 0, 0, 0)),
                  pl.BlockSpec((1, 1, MOBA_HEADS, 16, MOBA_BLOCK), lambda b, j: (b, j, 0, 0, 0))],
        out_specs=pl.BlockSpec((1, MOBA_W, MOBA_BLOCK), lambda b, j: (b, 0, j)),
        out_shape=jax.ShapeDtypeStruct((B, MOBA_W, S), BF16),
        scratch_shapes=[pltpu.VMEM((MOBA_HEADS, MOBA_HD, MOBA_BLOCK), F32),
                        pltpu.VMEM((MOBA_HEADS, 1, MOBA_BLOCK), F32),
                        pltpu.VMEM((MOBA_HEADS, 1, MOBA_BLOCK), F32)],
        compiler_params=_params("arbitrary", "arbitrary"),
        name="moba",
    )(qt, kh, vt, bias)


def _outproj_kernel(x_ref, og_ref, omt_ref, msm_ref, wo_ref, n2_ref, wq_ref, x1_ref, xn_ref, q_ref):
    om = (omt_ref[0].astype(F32).T * msm_ref[...]).astype(BF16)
    x1 = (x_ref[...]
          + jnp.dot(og_ref[...], wo_ref[0:GLA_V, :], preferred_element_type=F32)
          + jnp.dot(om, wo_ref[GLA_V:, :], preferred_element_type=F32))
    x1_ref[...] = x1
    ms = jnp.mean(x1 * x1, axis=-1, keepdims=True)
    xn = (x1 * lax.rsqrt(ms + EPS) * n2_ref[...]).astype(BF16)
    xn_ref[...] = xn
    q_ref[...] = jnp.dot(xn, wq_ref[...], preferred_element_type=F32).astype(BF16)


def _outproj(x2, o_gla, o_moba_t, ms_moba, w_out, norm2_w, wq, B, S):
    T = B * S
    nt = S // OUT_TILE
    QW = PEER_HEADS * PEER_QDIM
    return pl.pallas_call(
        _outproj_kernel,
        grid=(T // OUT_TILE,),
        in_specs=[pl.BlockSpec((OUT_TILE, D_MODEL), lambda i: (i, 0)),
                  pl.BlockSpec((OUT_TILE, GLA_V), lambda i: (i, 0)),
                  pl.BlockSpec((1, MOBA_W, OUT_TILE), lambda i: (i // nt, 0, i % nt)),
                  pl.BlockSpec((1, MOBA_W), lambda i: (0, 0)),
                  pl.BlockSpec((GLA_V + MOBA_W, D_MODEL), lambda i: (0, 0)),
                  pl.BlockSpec((1, D_MODEL), lambda i: (0, 0)),
                  pl.BlockSpec((D_MODEL, QW), lambda i: (0, 0))],
        out_specs=[pl.BlockSpec((OUT_TILE, D_MODEL), lambda i: (i, 0)),
                   pl.BlockSpec((OUT_TILE, D_MODEL), lambda i: (i, 0)),
                   pl.BlockSpec((OUT_TILE, QW), lambda i: (i, 0))],
        out_shape=[jax.ShapeDtypeStruct((T, D_MODEL), F32),
                   jax.ShapeDtypeStruct((T, D_MODEL), BF16),
                   jax.ShapeDtypeStruct((T, QW), BF16)],
        compiler_params=_params("arbitrary"),
        name="outproj",
    )(x2, o_gla, o_moba_t, ms_moba, w_out, norm2_w, wq)


_PAIRS = [(r, q) for r in range(PEER_TOPK + 1) for q in range(PEER_TOPK + 1)
          if (r + 1) * (q + 1) <= PEER_TOPK + 1]


def _tree(op, xs):
    xs = list(xs)
    while len(xs) > 1:
        xs = [op(xs[i], xs[i + 1]) if i + 1 < len(xs) else xs[i] for i in range(0, len(xs), 2)]
    return xs[0]


def _route_kernel(q_ref, sk_ref, rk_ref, e1_ref, cnt_ref, e0_ref, sc_ref, tmp_ref):
    H, NK = PEER_HEADS, PEER_NKEYS
    hq = H * PEER_QDIM // 2
    tile = q_ref.shape[0]
    G = 8
    for p in range(2):
        sc_ref[p] = _nt_dot(sk_ref[p], q_ref[:, p * hq:(p + 1) * hq])
    neg = jnp.full((H, tile), -jnp.inf, F32)

    def next_largest(belows):
        def body(g, ms):
            base = pl.multiple_of(g * (G * H), G * H)
            out = []
            for p in range(2):
                v = sc_ref[p, pl.ds(base, G * H), :]
                parts = [jnp.where(v[k * H:(k + 1) * H] < belows[p], v[k * H:(k + 1) * H], -jnp.inf)
                         for k in range(G)]
                out.append(jnp.maximum(ms[p], _tree(jnp.maximum, parts)))
            return tuple(out)
        return lax.fori_loop(0, NK // G, body, (neg, neg))

    tops = ([], [])
    belows = (jnp.full((H, tile), jnp.inf, F32),) * 2
    for _ in range(PEER_TOPK + 1):
        belows = next_largest(belows)
        tops[0].append(belows[0])
        tops[1].append(belows[1])
    cands = [tops[0][r] + tops[1][q] for r, q in _PAIRS]
    below = jnp.full((H, tile), jnp.inf, F32)
    for r in range(PEER_TOPK + 1):
        below = _tree(jnp.maximum, [jnp.where(c < below, c, -jnp.inf) for c in cands])
        if r == PEER_TOPK - 1:
            c16 = below
    tau = 0.5 * (c16 + below)
    top = tops[0][0] + tops[1][0]
    zinv = 1.0 / _tree(jnp.add, [jnp.where(c > tau, jnp.exp(c - top), 0.0) for c in cands])
    thr = [tau - b for b in tops[1]]

    def per_key(g, carry):
        base = pl.multiple_of(g * (G * H), G * H)
        v0 = sc_ref[0, pl.ds(base, G * H), :]
        v1 = sc_ref[1, pl.ds(base, G * H), :]
        for k in range(G):
            a = v0[k * H:(k + 1) * H]
            b = v1[k * H:(k + 1) * H]
            rows = pl.ds(pl.multiple_of(base + k * H, H), H)
            tmp_ref[0, rows, :] = _tree(jnp.add, [jnp.where(a > t, 1.0, 0.0) for t in thr])
            tmp_ref[1, rows, :] = jnp.exp(a - tops[0][0]) * zinv
            tmp_ref[2, rows, :] = _tree(jnp.add, [jnp.where(b <= t, 1.0, 0.0) for t in tops[1]])
            tmp_ref[3, rows, :] = jnp.exp(b - tops[1][0])
        return carry

    lax.fori_loop(0, NK // G, per_key, 0)
    for h in range(H):
        cnt_ref[h] = tmp_ref[0, pl.ds(h, NK, stride=H), :]
        e0_ref[h] = tmp_ref[1, pl.ds(h, NK, stride=H), :]
        rk_ref[h] = tmp_ref[2, pl.ds(h, NK, stride=H), :].astype(BF16)
        e1_ref[h] = tmp_ref[3, pl.ds(h, NK, stride=H), :].astype(BF16)


def _route(q, skbig):
    T = q.shape[0]
    H, NK = PEER_HEADS, PEER_NKEYS
    hq = H * PEER_QDIM // 2
    spec = pl.BlockSpec((H, NK, ROUTE_TILE), lambda i: (0, 0, i))
    return pl.pallas_call(
        _route_kernel,
        grid=(T // ROUTE_TILE,),
        in_specs=[pl.BlockSpec((ROUTE_TILE, 2 * hq), lambda i: (i, 0)),
                  pl.BlockSpec((2, NK * H, hq), lambda i: (0, 0, 0))],
        out_specs=[spec, spec, spec, spec],
        out_shape=[jax.ShapeDtypeStruct((H, NK, T), BF16), jax.ShapeDtypeStruct((H, NK, T), BF16),
                   jax.ShapeDtypeStruct((H, NK, T), F32), jax.ShapeDtypeStruct((H, NK, T), F32)],
        scratch_shapes=[pltpu.VMEM((2, NK * H, ROUTE_TILE), F32),
                        pltpu.VMEM((4, NK * H, ROUTE_TILE), F32)],
        compiler_params=_params("arbitrary"),
        name="peer_route",
    )(q, skbig)


PEER_NCHUNK = PEER_N // PEER_ECHUNK
PEER_STEPS = PEER_NCHUNK + 2


def _peer_step(par, c, xn_ref, u_ref, vt_ref, rk_ref, e1_ref, cnt_ref, e0_ref, acc_ref, act_refs, w_refs):
    NK = PEER_NKEYS
    nsub = PEER_ECHUNK // NK
    RB = PEER_STRIP_ROWS
    MB = PEER_MXU_BLOCK

    def stage1(mb, nh):
        rs, tk = slice(mb * MB, (mb + 1) * MB), slice(nh * MB, (nh + 1) * MB)
        act_refs[par][rs, tk] = _nt_dot(u_ref[rs, :], xn_ref[tk, :])

    def stage3(mb, nh):
        rs, tk = slice(mb * MB, (mb + 1) * MB), slice(nh * MB, (nh + 1) * MB)
        acc_ref[rs, tk] += jnp.dot(vt_ref[rs, :], w_refs[par][:, tk], preferred_element_type=F32)

    mxu_pieces = []
    for mb in range(PEER_ECHUNK // MB):
        for nh in range(PEER_TILE // MB):
            mxu_pieces.append(functools.partial(stage1, mb, nh))
            mxu_pieces.append(functools.partial(stage3, mb, nh))
    strips_per_piece = (nsub * (PEER_TILE // LANES) * (NK // RB)) // len(mxu_pieces)
    strip = 0
    act_ref, w_ref = act_refs[1 - par], w_refs[1 - par]
    cb = jnp.clip(c - 1, 0, PEER_NCHUNK - 1)
    for ii in range(nsub):
        i = cb * nsub + ii
        cnt_rows = [cnt_ref[h, pl.ds(i, 1), :] for h in range(PEER_HEADS)]
        e0_rows = [e0_ref[h, pl.ds(i, 1), :] for h in range(PEER_HEADS)]
        for lc in range(PEER_TILE // LANES):
            lanes = slice(lc * LANES, (lc + 1) * LANES)
            cnts = [jnp.broadcast_to(r[:, lanes], (RB, LANES)).astype(BF16) for r in cnt_rows]
            e0s = [jnp.broadcast_to(r[:, lanes], (RB, LANES)).astype(BF16) for r in e0_rows]
            for rb in range(NK // RB):
                if strip % strips_per_piece == 0:
                    mxu_pieces[strip // strips_per_piece]()
                strip += 1
                rows = slice(rb * RB, (rb + 1) * RB)
                wsum = _tree(jnp.add, [
                    e0s[h] * jnp.where(rk_ref[h, rows, lanes] <= cnts[h], e1_ref[h, rows, lanes], 0.0)
                    for h in range(PEER_HEADS)])
                er = slice(ii * NK + rb * RB, ii * NK + (rb + 1) * RB)
                a = act_ref[er, lanes]
                g = 0.5 * a * (1.0 + lax.erf(a * (1.0 / math.sqrt(2.0))))
                w_ref[er, lanes] = g.astype(BF16) * wsum


def _peer_kernel(xn_ref, x1_ref, u_ref, vt_ref, rk_ref, e1_ref, cnt_ref, e0_ref, o_ref,
                 acc_ref, act0_ref, act1_ref, w0_ref, w1_ref):
    c = pl.program_id(1)

    @pl.when(c == 0)
    def _():
        acc_ref[...] = jnp.zeros_like(acc_ref)
        act1_ref[...] = jnp.zeros_like(act1_ref)
        w0_ref[...] = jnp.zeros_like(w0_ref)

    for par in range(2):
        @pl.when(c % 2 == par)
        def _(par=par):
            _peer_step(par, c, xn_ref, u_ref, vt_ref, rk_ref, e1_ref, cnt_ref, e0_ref, acc_ref,
                       (act0_ref, act1_ref), (w0_ref, w1_ref))

    @pl.when(c == PEER_STEPS - 1)
    def _():
        o_ref[...] = x1_ref[...] + acc_ref[...].T


def _peer(xn, x1, u, vt, rk, e1, cnt, e0):
    T = xn.shape[0]
    H, NK = PEER_HEADS, PEER_NKEYS
    last = PEER_NCHUNK - 1
    rspec = pl.BlockSpec((H, NK, PEER_TILE), lambda t, c: (0, 0, t))
    return pl.pallas_call(
        _peer_kernel,
        grid=(T // PEER_TILE, PEER_STEPS),
        in_specs=[pl.BlockSpec((PEER_TILE, D_MODEL), lambda t, c: (t, 0)),
                  pl.BlockSpec((PEER_TILE, D_MODEL), lambda t, c: (t, 0)),
                  pl.BlockSpec((PEER_ECHUNK, D_MODEL), lambda t, c: (jnp.minimum(c, last), 0)),
                  pl.BlockSpec((D_MODEL, PEER_ECHUNK), lambda t, c: (0, jnp.clip(c - 2, 0, last))),
                  rspec, rspec, rspec, rspec],
        out_specs=pl.BlockSpec((PEER_TILE, D_MODEL), lambda t, c: (t, 0)),
        out_shape=jax.ShapeDtypeStruct((T, D_MODEL), F32),
        scratch_shapes=[pltpu.VMEM((D_MODEL, PEER_TILE), F32),
                        pltpu.VMEM((PEER_ECHUNK, PEER_TILE), F32),
                        pltpu.VMEM((PEER_ECHUNK, PEER_TILE), F32),
                        pltpu.VMEM((PEER_ECHUNK, PEER_TILE), BF16),
                        pltpu.VMEM((PEER_ECHUNK, PEER_TILE), BF16)],
        compiler_params=_params("arbitrary", "arbitrary"),
        name="peer",
    )(xn, x1, u, vt, rk, e1, cnt, e0)


def _layer(x, norm1_w, w_in, w_alpha, b_alpha, out_norm_w, qn_w, kn_w, mix_scale, w_out,
           norm2_w, w_query, subkeys, u_tab, v_tab):
    B, S, D = x.shape
    T = B * S
    x2 = x.reshape(T, D)

    o_gr = 2 * GLA_QK + 2 * GLA_V
    w_r = jnp.concatenate([w_in[:, :o_gr], w_in[:, o_gr + GLA_RANK:], w_in[:, o_gr:o_gr + GLA_RANK],
                           jnp.zeros((D, 128 - GLA_RANK), w_in.dtype)], axis=1).astype(BF16)
    wa = jnp.concatenate([w_alpha, jnp.zeros((128 - GLA_RANK, GLA_QK), w_alpha.dtype)], axis=0)

    proj = _inproj(x2, norm1_w.reshape(1, D), w_r)
    o_gla = _gla(proj, B, S, wa, b_alpha.reshape(1, GLA_QK), out_norm_w.reshape(1, GLA_DV),
                 mix_scale[:GLA_V].reshape(1, GLA_V))

    half = MOBA_HD // 2
    inv = ROPE_THETA ** (-jnp.arange(half, dtype=F32) / half)
    ang = inv[:, None] * jnp.arange(S, dtype=F32)[None, :]
    qw = jnp.broadcast_to(qn_w.reshape(MOBA_HD, 1), (MOBA_HD, MOBA_BLOCK))
    kw = jnp.broadcast_to(kn_w.reshape(MOBA_HD, 1), (MOBA_HD, MOBA_BLOCK))
    qt, kh, vt, bias = _moba_prep(proj, B, S, qw, kw, jnp.cos(ang), jnp.sin(ang))
    o_moba_t = _moba(qt, kh, vt, bias, B, S)

    H, NK, hd = PEER_HEADS, PEER_NKEYS, PEER_QDIM // 2
    wq = w_query.reshape(D, H, 2, hd).transpose(0, 2, 1, 3).reshape(D, 2 * H * hd).astype(BF16)
    eye = jnp.eye(H, dtype=subkeys.dtype)
    skbig = jnp.einsum('hpkd,hg->pkhgd', subkeys, eye).reshape(2, NK * H, H * hd).astype(BF16)

    x1, xn, q = _outproj(x2, o_gla, o_moba_t, mix_scale[GLA_V:].reshape(1, MOBA_W),
                         w_out.astype(BF16), norm2_w.reshape(1, D), wq, B, S)
    rk, e1, cnt, e0 = _route(q, skbig)
    out = _peer(xn, x1, u_tab.astype(BF16), v_tab.T.astype(BF16), rk, e1, cnt, e0)
    return out.reshape(B, S, D)


def kernel(x, norm1_w, w_in, gla_w_alpha, gla_b_alpha, gla_out_norm_w, moba_q_norm_w, moba_k_norm_w,
           mix_scale, w_out, norm2_w, peer_w_query, peer_subkeys, peer_u, peer_v):
    assert norm1_w.shape[0] == 1, "single-layer kernel"
    return _layer(x, norm1_w[0], w_in[0], gla_w_alpha[0], gla_b_alpha[0], gla_out_norm_w[0],
                  moba_q_norm_w[0], moba_k_norm_w[0], mix_scale[0], w_out[0], norm2_w[0],
                  peer_w_query[0], peer_subkeys[0], peer_u[0], peer_v[0])
```

```python
import functools
import math

import jax
import jax.numpy as jnp
from jax import lax
from jax.experimental import pallas as pl
from jax.experimental.pallas import tpu as pltpu

D_MODEL = 1024
GLA_HEADS, GLA_DK, GLA_DV, GLA_RANK, GLA_TAU, GLA_CHUNK = 4, 64, 128, 16, 16.0, 64
MOBA_HEADS, MOBA_HD, MOBA_BLOCK, MOBA_TOPK = 8, 64, 256, 3
ROPE_THETA = 10000.0
PEER_HEADS, PEER_NKEYS, PEER_QDIM, PEER_TOPK = 8, 128, 256, 16
PEER_N = PEER_NKEYS * PEER_NKEYS
EPS = 1e-6

GLA_QK = GLA_HEADS * GLA_DK
GLA_V = GLA_HEADS * GLA_DV
MOBA_W = MOBA_HEADS * MOBA_HD
PROJ_W = 2 * GLA_QK + 2 * GLA_V + 3 * MOBA_W + 128
GR_COLBLOCK = (PROJ_W - 128) // 128

F32 = jnp.float32
BF16 = jnp.bfloat16
HIGHEST = lax.Precision.HIGHEST
NEG = -1e30

VMEM_LIMIT_BYTES = 56 * 1024 * 1024

IN_TILE = 512
GLA_TILE = 256
OUT_TILE = 256
ROUTE_TILE = 256
ROUTE_LANES = 128
PEER_TILE = 512
PEER_ECHUNK = 1024
PEER_STRIP_ROWS = 32
PEER_MXU_BLOCK = 256
LANES = 256


def _params(*sem):
    return pltpu.CompilerParams(dimension_semantics=sem, vmem_limit_bytes=VMEM_LIMIT_BYTES)


def _nt_dot(a, b, precision=None):
    return lax.dot_general(a, b, (((1,), (1,)), ((), ())), precision=precision,
                           preferred_element_type=F32)


def _inproj_kernel(x_ref, nw_ref, w_ref, o_ref):
    x = x_ref[...]
    ms = jnp.mean(x * x, axis=-1, keepdims=True)
    xn = (x * lax.rsqrt(ms + EPS) * nw_ref[...]).astype(BF16)
    o_ref[...] = jnp.dot(xn, w_ref[...], preferred_element_type=F32)


def _inproj(x2, norm_w, w):
    T = x2.shape[0]
    return pl.pallas_call(
        _inproj_kernel,
        grid=(T // IN_TILE,),
        in_specs=[pl.BlockSpec((IN_TILE, D_MODEL), lambda i: (i, 0)),
                  pl.BlockSpec((1, D_MODEL), lambda i: (0, 0)),
                  pl.BlockSpec((D_MODEL, PROJ_W), lambda i: (0, 0))],
        out_specs=pl.BlockSpec((IN_TILE, PROJ_W), lambda i: (i, 0)),
        out_shape=jax.ShapeDtypeStruct((T, PROJ_W), F32),
        compiler_params=_params("arbitrary"),
        name="inproj",
    )(x2, norm_w, w)


def _gla_kernel(qk_ref, v_ref, gate_ref, gr_ref, wa_ref, ba_ref, nw_ref, ms_ref, o_ref, st_ref):
    @pl.when(pl.program_id(1) == 0)
    def _():
        st_ref[...] = jnp.zeros_like(st_ref)

    C = GLA_CHUNK
    z = jnp.dot(gr_ref[...], wa_ref[...], precision=HIGHEST, preferred_element_type=F32) + ba_ref[...]
    log_a = (jnp.minimum(z, 0.0) - jnp.log1p(jnp.exp(-jnp.abs(z)))) * (1.0 / GLA_TAU)
    row = lax.broadcasted_iota(jnp.int32, (C, C), 0)
    col = lax.broadcasted_iota(jnp.int32, (C, C), 1)
    causal = col <= row
    tril = causal.astype(F32)
    chunks = range(GLA_TILE // C)
    heads = range(GLA_HEADS)
    ks = [slice(h * GLA_DK, (h + 1) * GLA_DK) for h in heads]
    vs = [slice(h * GLA_DV, (h + 1) * GLA_DV) for h in heads]
    sl = [slice(c * C, (c + 1) * C) for c in chunks]
    q_d, k_in, k_st, decay = [], [], [], []
    for c in chunks:
        b = jnp.dot(tril, log_a[sl[c]], precision=HIGHEST, preferred_element_type=F32)
        b_last = b[C - 1:C, :]
        q = qk_ref[sl[c], 0:GLA_QK]
        k = qk_ref[sl[c], GLA_QK:2 * GLA_QK]
        q_d.append(q * jnp.exp(b) * (GLA_DK ** -0.5))
        k_in.append(k * jnp.exp(-b))
        k_st.append((k * jnp.exp(b_last - b)).astype(BF16))
        decay.append(jnp.exp(b_last))
    v = [[v_ref[sl[c], vs[h]] for h in heads] for c in chunks]
    a = [[jnp.where(causal, _nt_dot(q_d[c][:, ks[h]], k_in[c][:, ks[h]], precision=HIGHEST), 0.0)
          for h in heads] for c in chunks]
    o = [[jnp.dot(a[c][h].astype(BF16), v[c][h].astype(BF16), preferred_element_type=F32)
          for h in heads] for c in chunks]
    upd = [[jnp.dot(v[c][h].T.astype(BF16), k_st[c][:, ks[h]], preferred_element_type=F32)
            for h in heads] for c in chunks]
    st = [st_ref[h] for h in heads]
    for c in chunks:
        for h in heads:
            o[c][h] = o[c][h] + _nt_dot(q_d[c][:, ks[h]].astype(BF16), st[h].astype(BF16))
            st[h] = st[h] * decay[c][:, ks[h]] + upd[c][h]
    out = []
    for c in chunks:
        for h in heads:
            y = o[c][h]
            y = y * lax.rsqrt(jnp.mean(y * y, axis=-1, keepdims=True) + EPS) * nw_ref[...]
            g = gate_ref[sl[c], vs[h]]
            out.append((y * (g * jax.nn.sigmoid(g)) * ms_ref[:, vs[h]]).astype(BF16))
    for c in chunks:
        for h in heads:
            o_ref[sl[c], vs[h]] = out[c * GLA_HEADS + h]
    for h in heads:
        st_ref[h] = st[h]


def _gla(proj, B, S, w_alpha, b_alpha, out_norm_w, mix_scale_gla):
    nt = S // GLA_TILE
    tok = lambda b, j: b * nt + j
    return pl.pallas_call(
        _gla_kernel,
        grid=(B, nt),
        in_specs=[pl.BlockSpec((GLA_TILE, 2 * GLA_QK), lambda b, j: (tok(b, j), 0)),
                  pl.BlockSpec((GLA_TILE, GLA_V), lambda b, j: (tok(b, j), 1)),
                  pl.BlockSpec((GLA_TILE, GLA_V), lambda b, j: (tok(b, j), 2)),
                  pl.BlockSpec((GLA_TILE, 128), lambda b, j: (tok(b, j), GR_COLBLOCK)),
                  pl.BlockSpec((128, GLA_QK), lambda b, j: (0, 0)),
                  pl.BlockSpec((1, GLA_QK), lambda b, j: (0, 0)),
                  pl.BlockSpec((1, GLA_DV), lambda b, j: (0, 0)),
                  pl.BlockSpec((1, GLA_V), lambda b, j: (0, 0))],
        out_specs=pl.BlockSpec((GLA_TILE, GLA_V), lambda b, j: (tok(b, j), 0)),
        out_shape=jax.ShapeDtypeStruct((B * S, GLA_V), BF16),
        scratch_shapes=[pltpu.VMEM((GLA_HEADS, GLA_DV, GLA_DK), F32)],
        compiler_params=_params("arbitrary", "arbitrary"),
        name="gla",
    )(proj, proj, proj, proj, w_alpha, b_alpha, out_norm_w, mix_scale_gla)


def _norm_rope_t(xt, w_ref, cos_ref, sin_ref):
    half = MOBA_HD // 2
    outs = []
    for h in range(MOBA_HEADS):
        xh = xt[h * MOBA_HD:(h + 1) * MOBA_HD]
        ms = jnp.mean(xh * xh, axis=0, keepdims=True)
        y = xh * lax.rsqrt(ms + EPS) * w_ref[...]
        y1, y2 = y[:half], y[half:]
        c, s = cos_ref[...], sin_ref[...]
        outs.append(y1 * c - y2 * s)
        outs.append(y2 * c + y1 * s)
    return jnp.concatenate(outs, axis=0)


def _moba_prep_kernel(q_ref, k_ref, v_ref, qw_ref, kw_ref, cos_ref, sin_ref,
                      qt_ref, kh_ref, vt_ref, bias_ref, mt_ref):
    j = pl.program_id(1)
    nb = 16

    @pl.when(j == 0)
    def _():
        mt_ref[...] = jnp.zeros_like(mt_ref)

    qt = _norm_rope_t(q_ref[...].T, qw_ref, cos_ref, sin_ref)
    kt = _norm_rope_t(k_ref[...].T, kw_ref, cos_ref, sin_ref)
    qt_ref[0] = (qt * (MOBA_HD ** -0.5)).astype(BF16)
    kn = kt.T
    for h in range(MOBA_HEADS):
        kh_ref[0, h] = kn[:, h * MOBA_HD:(h + 1) * MOBA_HD].astype(BF16)
    vt_ref[0, 0] = v_ref[...].T.astype(BF16)

    bs = jnp.dot(mt_ref[...], qt, precision=HIGHEST, preferred_element_type=F32)
    n_idx = lax.broadcasted_iota(jnp.int32, (nb, MOBA_BLOCK), 0)
    for h in range(MOBA_HEADS):
        sc = jnp.where(n_idx < j, bs[h * nb:(h + 1) * nb], -jnp.inf)
        sel = jnp.zeros((nb, MOBA_BLOCK), jnp.bool_)
        for r in range(MOBA_TOPK):
            m = jnp.max(sc, axis=0, keepdims=True)
            first = jnp.min(jnp.where(sc == m, n_idx, nb), axis=0, keepdims=True)
            hit = n_idx == first
            sel = jnp.logical_or(sel, jnp.logical_and(hit, r < j))
            sc = jnp.where(hit, -jnp.inf, sc)
        bias_ref[0, 0, h] = jnp.where(sel, 0.0, NEG)

    kbar = jnp.mean(kn, axis=0, keepdims=True)
    r_idx = lax.broadcasted_iota(jnp.int32, (MOBA_HEADS * nb, MOBA_W), 0)
    c_idx = lax.broadcasted_iota(jnp.int32, (MOBA_HEADS * nb, MOBA_W), 1)
    mine = jnp.logical_and(r_idx % nb == j, r_idx // nb == c_idx // MOBA_HD)
    mt_ref[...] = jnp.where(mine, kbar, mt_ref[...])


def _moba_prep(proj, B, S, qw, kw, cos_t, sin_t):
    nb = S // MOBA_BLOCK
    assert nb <= 16
    tok = lambda b, j: b * nb + j
    half = MOBA_HD // 2
    return pl.pallas_call(
        _moba_prep_kernel,
        grid=(B, nb),
        in_specs=[pl.BlockSpec((MOBA_BLOCK, MOBA_W), lambda b, j: (tok(b, j), 3)),
                  pl.BlockSpec((MOBA_BLOCK, MOBA_W), lambda b, j: (tok(b, j), 4)),
                  pl.BlockSpec((MOBA_BLOCK, MOBA_W), lambda b, j: (tok(b, j), 5)),
                  pl.BlockSpec((MOBA_HD, MOBA_BLOCK), lambda b, j: (0, 0)),
                  pl.BlockSpec((MOBA_HD, MOBA_BLOCK), lambda b, j: (0, 0)),
                  pl.BlockSpec((half, MOBA_BLOCK), lambda b, j: (0, j)),
                  pl.BlockSpec((half, MOBA_BLOCK), lambda b, j: (0, j))],
        out_specs=[pl.BlockSpec((1, MOBA_W, MOBA_BLOCK), lambda b, j: (b, 0, j)),
                   pl.BlockSpec((1, MOBA_HEADS, MOBA_BLOCK, MOBA_HD), lambda b, j: (b, 0, j, 0)),
                   pl.BlockSpec((1, 1, MOBA_W, MOBA_BLOCK), lambda b, j: (b, j, 0, 0)),
                   pl.BlockSpec((1, 1, MOBA_HEADS, 16, MOBA_BLOCK), lambda b, j: (b, j, 0, 0, 0))],
        out_shape=[jax.ShapeDtypeStruct((B, MOBA_W, S), BF16),
                   jax.ShapeDtypeStruct((B, MOBA_HEADS, S, MOBA_HD), BF16),
                   jax.ShapeDtypeStruct((B, nb, MOBA_W, MOBA_BLOCK), BF16),
                   jax.ShapeDtypeStruct((B, nb, MOBA_HEADS, 16, MOBA_BLOCK), F32)],
        scratch_shapes=[pltpu.VMEM((MOBA_HEADS * 16, MOBA_W), F32)],
        compiler_params=_params("arbitrary", "arbitrary"),
        name="moba_prep",
    )(proj, proj, proj, qw, kw, cos_t, sin_t)


def _moba_kernel(qt_ref, k_ref, vt_ref, bias_ref, o_ref, acc_ref, m_ref, l_ref):
    j = pl.program_id(1)
    hd = MOBA_HD
    kpos = lax.broadcasted_iota(jnp.int32, (MOBA_BLOCK, MOBA_BLOCK), 0)
    qpos = lax.broadcasted_iota(jnp.int32, (MOBA_BLOCK, MOBA_BLOCK), 1)
    causal = kpos <= qpos
    own = pl.ds(pl.multiple_of(j * MOBA_BLOCK, MOBA_BLOCK), MOBA_BLOCK)
    heads = range(MOBA_HEADS)
    rows = [slice(h * hd, (h + 1) * hd) for h in heads]
    s = [jnp.where(causal, jnp.dot(k_ref[0, h, own, :], qt_ref[0, rows[h], :], preferred_element_type=F32),
                   -jnp.inf) for h in heads]
    m = [jnp.max(s[h], axis=0, keepdims=True) for h in heads]
    p = [jnp.exp(s[h] - m[h]) for h in heads]
    l = [jnp.sum(p[h], axis=0, keepdims=True) for h in heads]
    pv = [jnp.dot(vt_ref[0, j, rows[h], :], p[h].astype(BF16), preferred_element_type=F32)
          for h in heads]
    for h in heads:
        m_ref[h] = m[h]
        l_ref[h] = l[h]
        acc_ref[h] = pv[h]

    def body(n, carry):
        blk = pl.ds(pl.multiple_of(n * MOBA_BLOCK, MOBA_BLOCK), MOBA_BLOCK)
        s = [jnp.dot(k_ref[0, h, blk, :], qt_ref[0, rows[h], :], preferred_element_type=F32)
             + bias_ref[0, 0, h, pl.ds(n, 1), :] for h in heads]
        m_old = [m_ref[h] for h in heads]
        m_new = [jnp.maximum(m_old[h], jnp.max(s[h], axis=0, keepdims=True)) for h in heads]
        alpha = [jnp.exp(m_old[h] - m_new[h]) for h in heads]
        p = [jnp.exp(s[h] - m_new[h]) for h in heads]
        l_new = [alpha[h] * l_ref[h] + jnp.sum(p[h], axis=0, keepdims=True) for h in heads]
        pv = [jnp.dot(vt_ref[0, n, rows[h], :], p[h].astype(BF16), preferred_element_type=F32)
              for h in heads]
        acc_new = [acc_ref[h] * alpha[h] + pv[h] for h in heads]
        for h in heads:
            m_ref[h] = m_new[h]
            l_ref[h] = l_new[h]
            acc_ref[h] = acc_new[h]
        return carry

    lax.fori_loop(0, j, body, 0)
    for h in range(MOBA_HEADS):
        o_ref[0, h * hd:(h + 1) * hd, :] = (acc_ref[h] / l_ref[h]).astype(BF16)


def _moba(qt, kh, vt, bias, B, S):
    nb = S // MOBA_BLOCK
    return pl.pallas_call(
        _moba_kernel,
        grid=(B, nb),
        in_specs=[pl.BlockSpec((1, MOBA_W, MOBA_BLOCK), lambda b, j: (b, 0, j)),
                  pl.BlockSpec((1, MOBA_HEADS, S, MOBA_HD), lambda b, j: (b, 0, 0, 0)),
                  pl.BlockSpec((1, nb, MOBA_W, MOBA_BLOCK), lambda b, j: (b, 0, 0, 0)),
                  pl.BlockSpec((1, 1, MOBA_HEADS, 16, MOBA_BLOCK), lambda b, j: (b, j, 0, 0, 0))],
        out_specs=pl.BlockSpec((1, MOBA_W, MOBA_BLOCK), lambda b, j: (b, 0, j)),
        out_shape=jax.ShapeDtypeStruct((B, MOBA_W, S), BF16),
        scratch_shapes=[pltpu.VMEM((MOBA_HEADS, MOBA_HD, MOBA_BLOCK), F32),
                        pltpu.VMEM((MOBA_HEADS, 1, MOBA_BLOCK), F32),
                        pltpu.VMEM((MOBA_HEADS, 1, MOBA_BLOCK), F32)],
        compiler_params=_params("arbitrary", "arbitrary"),
        name="moba",
    )(qt, kh, vt, bias)


def _outproj_kernel(x_ref, og_ref, omt_ref, msm_ref, wo_ref, n2_ref, wq_ref, x1_ref, xn_ref, q_ref):
    om = (omt_ref[0].astype(F32).T * msm_ref[...]).astype(BF16)
    x1 = (x_ref[...]
          + jnp.dot(og_ref[...], wo_ref[0:GLA_V, :], preferred_element_type=F32)
          + jnp.dot(om, wo_ref[GLA_V:, :], preferred_element_type=F32))
    x1_ref[...] = x1
    ms = jnp.mean(x1 * x1, axis=-1, keepdims=True)
    xn = (x1 * lax.rsqrt(ms + EPS) * n2_ref[...]).astype(BF16)
    xn_ref[...] = xn
    q_ref[...] = jnp.dot(xn, wq_ref[...], preferred_element_type=F32).astype(BF16)


def _outproj(x2, o_gla, o_moba_t, ms_moba, w_out, norm2_w, wq, B, S):
    T = B * S
    nt = S // OUT_TILE
    QW = PEER_HEADS * PEER_QDIM
    return pl.pallas_call(
        _outproj_kernel,
        grid=(T // OUT_TILE,),
        in_specs=[pl.BlockSpec((OUT_TILE, D_MODEL), lambda i: (i, 0)),
                  pl.BlockSpec((OUT_TILE, GLA_V), lambda i: (i, 0)),
                  pl.BlockSpec((1, MOBA_W, OUT_TILE), lambda i: (i // nt, 0, i % nt)),
                  pl.BlockSpec((1, MOBA_W), lambda i: (0, 0)),
                  pl.BlockSpec((GLA_V + MOBA_W, D_MODEL), lambda i: (0, 0)),
                  pl.BlockSpec((1, D_MODEL), lambda i: (0, 0)),
                  pl.BlockSpec((D_MODEL, QW), lambda i: (0, 0))],
        out_specs=[pl.BlockSpec((OUT_TILE, D_MODEL), lambda i: (i, 0)),
                   pl.BlockSpec((OUT_TILE, D_MODEL), lambda i: (i, 0)),
                   pl.BlockSpec((OUT_TILE, QW), lambda i: (i, 0))],
        out_shape=[jax.ShapeDtypeStruct((T, D_MODEL), F32),
                   jax.ShapeDtypeStruct((T, D_MODEL), BF16),
                   jax.ShapeDtypeStruct((T, QW), BF16)],
        compiler_params=_params("arbitrary"),
        name="outproj",
    )(x2, o_gla, o_moba_t, ms_moba, w_out, norm2_w, wq)


_PAIRS = [(r, q) for r in range(PEER_TOPK + 1) for q in range(PEER_TOPK + 1)
          if (r + 1) * (q + 1) <= PEER_TOPK + 1]


def _tree(op, xs):
    xs = list(xs)
    while len(xs) > 1:
        xs = [op(xs[i], xs[i + 1]) if i + 1 < len(xs) else xs[i] for i in range(0, len(xs), 2)]
    return xs[0]


def _route_kernel(q_ref, sk_ref, rk_ref, e1_ref, cnt_ref, e0_ref, sc_ref, tmp_ref):
    H, NK = PEER_HEADS, PEER_NKEYS
    hq = H * PEER_QDIM // 2
    G = 8
    NTOP = PEER_TOPK + 1
    for p in range(2):
        sc = _nt_dot(sk_ref[p], q_ref[:, p * hq:(p + 1) * hq])
        for lt in range(ROUTE_TILE // ROUTE_LANES):
            sc_ref[p, lt] = sc[:, lt * ROUTE_LANES:(lt + 1) * ROUTE_LANES]
    neg = jnp.full((H, ROUTE_LANES), -jnp.inf, F32)

    def insert(top, v):
        out = []
        for t in top:
            out.append(jnp.maximum(t, v))
            v = jnp.minimum(t, v)
        return out

    for lt in range(ROUTE_TILE // ROUTE_LANES):
        lanes = slice(lt * ROUTE_LANES, (lt + 1) * ROUTE_LANES)

        def largest(p, lt=lt):
            def body(g, top):
                v = sc_ref[p, lt, pl.ds(pl.multiple_of(g * (G * H), G * H), G * H), :]
                top = list(top)
                for k in range(G):
                    top = insert(top, v[k * H:(k + 1) * H])
                return tuple(top)
            return lax.fori_loop(0, NK // G, body, (neg,) * NTOP)

        tops = (largest(0), largest(1))
        best = [neg] * NTOP
        for r, q in _PAIRS:
            best = insert(best, tops[0][r] + tops[1][q])
        tau = 0.5 * (best[PEER_TOPK - 1] + best[PEER_TOPK])
        top = tops[0][0] + tops[1][0]
        zinv = 1.0 / _tree(jnp.add, [jnp.where(tops[0][r] + tops[1][q] > tau,
                                               jnp.exp(tops[0][r] + tops[1][q] - top), 0.0)
                                     for r, q in _PAIRS])
        thr = [tau - b for b in tops[1]]

        def per_key(g, carry, lt=lt, tops=tops, thr=thr, zinv=zinv):
            base = pl.multiple_of(g * (G * H), G * H)
            v0 = sc_ref[0, lt, pl.ds(base, G * H), :]
            v1 = sc_ref[1, lt, pl.ds(base, G * H), :]
            for k in range(G):
                a = v0[k * H:(k + 1) * H]
                b = v1[k * H:(k + 1) * H]
                rows = pl.ds(pl.multiple_of(base + k * H, H), H)
                tmp_ref[0, rows, :] = _tree(jnp.add, [jnp.where(a > t, 1.0, 0.0) for t in thr])
                tmp_ref[1, rows, :] = jnp.exp(a - tops[0][0]) * zinv
                tmp_ref[2, rows, :] = _tree(jnp.add, [jnp.where(b <= t, 1.0, 0.0) for t in tops[1]])
                tmp_ref[3, rows, :] = jnp.exp(b - tops[1][0])
            return carry

        lax.fori_loop(0, NK // G, per_key, 0)
        for h in range(H):
            cnt_ref[h, :, lanes] = tmp_ref[0, pl.ds(h, NK, stride=H), :]
            e0_ref[h, :, lanes] = tmp_ref[1, pl.ds(h, NK, stride=H), :]
            rk_ref[h, :, lanes] = tmp_ref[2, pl.ds(h, NK, stride=H), :].astype(BF16)
            e1_ref[h, :, lanes] = tmp_ref[3, pl.ds(h, NK, stride=H), :].astype(BF16)


def _route(q, skbig):
    T = q.shape[0]
    H, NK = PEER_HEADS, PEER_NKEYS
    hq = H * PEER_QDIM // 2
    spec = pl.BlockSpec((H, NK, ROUTE_TILE), lambda i: (0, 0, i))
    return pl.pallas_call(
        _route_kernel,
        grid=(T // ROUTE_TILE,),
        in_specs=[pl.BlockSpec((ROUTE_TILE, 2 * hq), lambda i: (i, 0)),
                  pl.BlockSpec((2, NK * H, hq), lambda i: (0, 0, 0))],
        out_specs=[spec, spec, spec, spec],
        out_shape=[jax.ShapeDtypeStruct((H, NK, T), BF16), jax.ShapeDtypeStruct((H, NK, T), BF16),
                   jax.ShapeDtypeStruct((H, NK, T), F32), jax.ShapeDtypeStruct((H, NK, T), F32)],
        scratch_shapes=[pltpu.VMEM((2, ROUTE_TILE // ROUTE_LANES, NK * H, ROUTE_LANES), F32),
                        pltpu.VMEM((4, NK * H, ROUTE_LANES), F32)],
        compiler_params=_params("arbitrary"),
        name="peer_route",
    )(q, skbig)


PEER_NCHUNK = PEER_N // PEER_ECHUNK
PEER_STEPS = PEER_NCHUNK + 2


def _peer_step(par, c, xn_ref, u_ref, vt_ref, rk_ref, e1_ref, cnt_ref, e0_ref, acc_ref, act_refs, w_refs):
    NK = PEER_NKEYS
    nsub = PEER_ECHUNK // NK
    RB = PEER_STRIP_ROWS
    MB = PEER_MXU_BLOCK

    def stage1(mb, nh):
        rs, tk = slice(mb * MB, (mb + 1) * MB), slice(nh * MB, (nh + 1) * MB)
        act_refs[par][rs, tk] = _nt_dot(u_ref[rs, :], xn_ref[tk, :])

    def stage3(mb, nh):
        rs, tk = slice(mb * MB, (mb + 1) * MB), slice(nh * MB, (nh + 1) * MB)
        acc_ref[rs, tk] += jnp.dot(vt_ref[0, rs, :], w_refs[par][:, tk], preferred_element_type=F32)

    mxu_pieces = []
    for mb in range(PEER_ECHUNK // MB):
        for nh in range(PEER_TILE // MB):
            mxu_pieces.append(functools.partial(stage1, mb, nh))
            mxu_pieces.append(functools.partial(stage3, mb, nh))
    strips_per_piece = (nsub * (PEER_TILE // LANES) * (NK // RB)) // len(mxu_pieces)
    strip = 0
    act_ref, w_ref = act_refs[1 - par], w_refs[1 - par]
    cb = jnp.clip(c - 1, 0, PEER_NCHUNK - 1)
    for ii in range(nsub):
        i = cb * nsub + ii
        cnt_rows = [cnt_ref[h, pl.ds(i, 1), :] for h in range(PEER_HEADS)]
        e0_rows = [e0_ref[h, pl.ds(i, 1), :] for h in range(PEER_HEADS)]
        for lc in range(PEER_TILE // LANES):
            lanes = slice(lc * LANES, (lc + 1) * LANES)
            cnts = [jnp.broadcast_to(r[:, lanes], (RB, LANES)).astype(BF16) for r in cnt_rows]
            e0s = [jnp.broadcast_to(r[:, lanes], (RB, LANES)).astype(BF16) for r in e0_rows]
            for rb in range(NK // RB):
                if strip % strips_per_piece == 0:
                    mxu_pieces[strip // strips_per_piece]()
                strip += 1
                rows = slice(rb * RB, (rb + 1) * RB)
                wsum = _tree(jnp.add, [
                    e0s[h] * jnp.where(rk_ref[h, rows, lanes] <= cnts[h], e1_ref[h, rows, lanes], 0.0)
                    for h in range(PEER_HEADS)])
                er = slice(ii * NK + rb * RB, ii * NK + (rb + 1) * RB)
                a = act_ref[er, lanes]
                g = 0.5 * a * (1.0 + lax.erf(a * (1.0 / math.sqrt(2.0))))
                w_ref[er, lanes] = g.astype(BF16) * wsum


def _peer_kernel(xn_ref, x1_ref, u_ref, vt_ref, rk_ref, e1_ref, cnt_ref, e0_ref, o_ref,
                 acc_ref, act0_ref, act1_ref, w0_ref, w1_ref):
    c = pl.program_id(1)

    @pl.when(c == 0)
    def _():
        acc_ref[...] = jnp.zeros_like(acc_ref)
        act1_ref[...] = jnp.zeros_like(act1_ref)
        w0_ref[...] = jnp.zeros_like(w0_ref)

    for par in range(2):
        @pl.when(c % 2 == par)
        def _(par=par):
            _peer_step(par, c, xn_ref, u_ref, vt_ref, rk_ref, e1_ref, cnt_ref, e0_ref, acc_ref,
                       (act0_ref, act1_ref), (w0_ref, w1_ref))

    @pl.when(c == PEER_STEPS - 1)
    def _():
        o_ref[...] = x1_ref[...] + acc_ref[...].T


def _peer(xn, x1, u, vt, rk, e1, cnt, e0):
    T = xn.shape[0]
    H, NK = PEER_HEADS, PEER_NKEYS
    last = PEER_NCHUNK - 1
    rspec = pl.BlockSpec((H, NK, PEER_TILE), lambda t, c: (0, 0, t))
    return pl.pallas_call(
        _peer_kernel,
        grid=(T // PEER_TILE, PEER_STEPS),
        in_specs=[pl.BlockSpec((PEER_TILE, D_MODEL), lambda t, c: (t, 0)),
                  pl.BlockSpec((PEER_TILE, D_MODEL), lambda t, c: (t, 0)),
                  pl.BlockSpec((PEER_ECHUNK, D_MODEL), lambda t, c: (jnp.minimum(c, last), 0)),
                  pl.BlockSpec((1, D_MODEL, PEER_ECHUNK), lambda t, c: (jnp.clip(c - 2, 0, last), 0, 0)),
                  rspec, rspec, rspec, rspec],
        out_specs=pl.BlockSpec((PEER_TILE, D_MODEL), lambda t, c: (t, 0)),
        out_shape=jax.ShapeDtypeStruct((T, D_MODEL), F32),
        scratch_shapes=[pltpu.VMEM((D_MODEL, PEER_TILE), F32),
                        pltpu.VMEM((PEER_ECHUNK, PEER_TILE), F32),
                        pltpu.VMEM((PEER_ECHUNK, PEER_TILE), F32),
                        pltpu.VMEM((PEER_ECHUNK, PEER_TILE), BF16),
                        pltpu.VMEM((PEER_ECHUNK, PEER_TILE), BF16)],
        compiler_params=_params("arbitrary", "arbitrary"),
        name="peer",
    )(xn, x1, u, vt, rk, e1, cnt, e0)


def _layer(x, norm1_w, w_in, w_alpha, b_alpha, out_norm_w, qn_w, kn_w, mix_scale, w_out,
           norm2_w, w_query, subkeys, u_tab, v_tab):
    B, S, D = x.shape
    T = B * S
    x2 = x.reshape(T, D)

    o_gr = 2 * GLA_QK + 2 * GLA_V
    w_r = jnp.concatenate([w_in[:, :o_gr], w_in[:, o_gr + GLA_RANK:], w_in[:, o_gr:o_gr + GLA_RANK],
                           jnp.zeros((D, 128 - GLA_RANK), w_in.dtype)], axis=1).astype(BF16)
    wa = jnp.concatenate([w_alpha, jnp.zeros((128 - GLA_RANK, GLA_QK), w_alpha.dtype)], axis=0)

    proj = _inproj(x2, norm1_w.reshape(1, D), w_r)
    o_gla = _gla(proj, B, S, wa, b_alpha.reshape(1, GLA_QK), out_norm_w.reshape(1, GLA_DV),
                 mix_scale[:GLA_V].reshape(1, GLA_V))

    half = MOBA_HD // 2
    inv = ROPE_THETA ** (-jnp.arange(half, dtype=F32) / half)
    ang = inv[:, None] * jnp.arange(S, dtype=F32)[None, :]
    qw = jnp.broadcast_to(qn_w.reshape(MOBA_HD, 1), (MOBA_HD, MOBA_BLOCK))
    kw = jnp.broadcast_to(kn_w.reshape(MOBA_HD, 1), (MOBA_HD, MOBA_BLOCK))
    qt, kh, vt, bias = _moba_prep(proj, B, S, qw, kw, jnp.cos(ang), jnp.sin(ang))
    o_moba_t = _moba(qt, kh, vt, bias, B, S)

    H, NK, hd = PEER_HEADS, PEER_NKEYS, PEER_QDIM // 2
    wq = w_query.reshape(D, H, 2, hd).transpose(0, 2, 1, 3).reshape(D, 2 * H * hd).astype(BF16)
    eye = jnp.eye(H, dtype=subkeys.dtype)
    skbig = jnp.einsum('hpkd,hg->pkhgd', subkeys, eye).reshape(2, NK * H, H * hd).astype(BF16)

    x1, xn, q = _outproj(x2, o_gla, o_moba_t, mix_scale[GLA_V:].reshape(1, MOBA_W),
                         w_out.astype(BF16), norm2_w.reshape(1, D), wq, B, S)
    rk, e1, cnt, e0 = _route(q, skbig)
    vt = v_tab.reshape(PEER_NCHUNK, PEER_ECHUNK, D).transpose(0, 2, 1).astype(BF16)
    out = _peer(xn, x1, u_tab.astype(BF16), vt, rk, e1, cnt, e0)
    return out.reshape(B, S, D)


def kernel(x, norm1_w, w_in, gla_w_alpha, gla_b_alpha, gla_out_norm_w, moba_q_norm_w, moba_k_norm_w,
           mix_scale, w_out, norm2_w, peer_w_query, peer_subkeys, peer_u, peer_v):
    assert norm1_w.shape[0] == 1, "single-layer kernel"
    return _layer(x, norm1_w[0], w_in[0], gla_w_alpha[0], gla_b_alpha[0], gla_out_norm_w[0],
                  moba_q_norm_w[0], moba_k_norm_w[0], mix_scale[0], w_out[0], norm2_w[0],
                  peer_w_query[0], peer_subkeys[0], peer_u[0], peer_v[0])
```

```python
import math

import jax
import jax.numpy as jnp
from jax import lax
from jax.experimental import pallas as pl
from jax.experimental.pallas import tpu as pltpu

D_MODEL = 1024
GLA_HEADS, GLA_DK, GLA_DV, GLA_RANK, GLA_TAU, GLA_CHUNK = 4, 64, 128, 16, 16.0, 64
MOBA_HEADS, MOBA_HD, MOBA_BLOCK, MOBA_TOPK = 8, 64, 256, 3
ROPE_THETA = 10000.0
PEER_HEADS, PEER_NKEYS, PEER_QDIM, PEER_TOPK = 8, 128, 256, 16
PEER_N = PEER_NKEYS * PEER_NKEYS
EPS = 1e-6

GLA_QK = GLA_HEADS * GLA_DK
GLA_V = GLA_HEADS * GLA_DV
MOBA_W = MOBA_HEADS * MOBA_HD
PROJ_W = 2 * GLA_QK + 2 * GLA_V + 3 * MOBA_W + 128
GR_COLBLOCK = (PROJ_W - 128) // 128

F32 = jnp.float32
BF16 = jnp.bfloat16
HIGHEST = lax.Precision.HIGHEST
NEG = -1e30

VMEM_LIMIT_BYTES = 56 * 1024 * 1024

IN_TILE = 512
GLA_TILE = 256
OUT_TILE = 256
ROUTE_TILE = 256
ROUTE_LANES = 128
PEER_TILE = 512
PEER_ECHUNK = 1024
PEER_STRIP_ROWS = 32
LANES = 256


def _params(*sem):
    return pltpu.CompilerParams(dimension_semantics=sem, vmem_limit_bytes=VMEM_LIMIT_BYTES)


def _nt_dot(a, b, precision=None):
    return lax.dot_general(a, b, (((1,), (1,)), ((), ())), precision=precision,
                           preferred_element_type=F32)


def _inproj_kernel(x_ref, nw_ref, w_ref, o_ref):
    x = x_ref[...]
    ms = jnp.mean(x * x, axis=-1, keepdims=True)
    xn = (x * lax.rsqrt(ms + EPS) * nw_ref[...]).astype(BF16)
    o_ref[...] = jnp.dot(xn, w_ref[...], preferred_element_type=F32)


def _inproj(x2, norm_w, w):
    T = x2.shape[0]
    return pl.pallas_call(
        _inproj_kernel,
        grid=(T // IN_TILE,),
        in_specs=[pl.BlockSpec((IN_TILE, D_MODEL), lambda i: (i, 0)),
                  pl.BlockSpec((1, D_MODEL), lambda i: (0, 0)),
                  pl.BlockSpec((D_MODEL, PROJ_W), lambda i: (0, 0))],
        out_specs=pl.BlockSpec((IN_TILE, PROJ_W), lambda i: (i, 0)),
        out_shape=jax.ShapeDtypeStruct((T, PROJ_W), F32),
        compiler_params=_params("arbitrary"),
        name="inproj",
    )(x2, norm_w, w)


def _gla_kernel(qk_ref, v_ref, gate_ref, gr_ref, wa_ref, ba_ref, nw_ref, ms_ref, o_ref, st_ref):
    @pl.when(pl.program_id(1) == 0)
    def _():
        st_ref[...] = jnp.zeros_like(st_ref)

    C = GLA_CHUNK
    z = jnp.dot(gr_ref[...], wa_ref[...], precision=HIGHEST, preferred_element_type=F32) + ba_ref[...]
    log_a = (jnp.minimum(z, 0.0) - jnp.log1p(jnp.exp(-jnp.abs(z)))) * (1.0 / GLA_TAU)
    row = lax.broadcasted_iota(jnp.int32, (C, C), 0)
    col = lax.broadcasted_iota(jnp.int32, (C, C), 1)
    causal = col <= row
    tril = causal.astype(F32)
    chunks = range(GLA_TILE // C)
    heads = range(GLA_HEADS)
    ks = [slice(h * GLA_DK, (h + 1) * GLA_DK) for h in heads]
    vs = [slice(h * GLA_DV, (h + 1) * GLA_DV) for h in heads]
    sl = [slice(c * C, (c + 1) * C) for c in chunks]
    q_d, k_in, k_st, decay = [], [], [], []
    for c in chunks:
        b = jnp.dot(tril, log_a[sl[c]], precision=HIGHEST, preferred_element_type=F32)
        b_last = b[C - 1:C, :]
        q = qk_ref[sl[c], 0:GLA_QK]
        k = qk_ref[sl[c], GLA_QK:2 * GLA_QK]
        q_d.append(q * jnp.exp(b) * (GLA_DK ** -0.5))
        k_in.append(k * jnp.exp(-b))
        k_st.append((k * jnp.exp(b_last - b)).astype(BF16))
        decay.append(jnp.exp(b_last))
    v = [[v_ref[sl[c], vs[h]] for h in heads] for c in chunks]
    a = [[jnp.where(causal, _nt_dot(q_d[c][:, ks[h]], k_in[c][:, ks[h]], precision=HIGHEST), 0.0)
          for h in heads] for c in chunks]
    o = [[jnp.dot(a[c][h].astype(BF16), v[c][h].astype(BF16), preferred_element_type=F32)
          for h in heads] for c in chunks]
    upd = [[jnp.dot(v[c][h].T.astype(BF16), k_st[c][:, ks[h]], preferred_element_type=F32)
            for h in heads] for c in chunks]
    st = [st_ref[h] for h in heads]
    for c in chunks:
        for h in heads:
            o[c][h] = o[c][h] + _nt_dot(q_d[c][:, ks[h]].astype(BF16), st[h].astype(BF16))
            st[h] = st[h] * decay[c][:, ks[h]] + upd[c][h]
    out = []
    for c in chunks:
        for h in heads:
            y = o[c][h]
            y = y * lax.rsqrt(jnp.mean(y * y, axis=-1, keepdims=True) + EPS) * nw_ref[...]
            g = gate_ref[sl[c], vs[h]]
            out.append((y * (g * jax.nn.sigmoid(g)) * ms_ref[:, vs[h]]).astype(BF16))
    for c in chunks:
        for h in heads:
            o_ref[sl[c], vs[h]] = out[c * GLA_HEADS + h]
    for h in heads:
        st_ref[h] = st[h]


def _gla(proj, B, S, w_alpha, b_alpha, out_norm_w, mix_scale_gla):
    nt = S // GLA_TILE
    tok = lambda b, j: b * nt + j
    return pl.pallas_call(
        _gla_kernel,
        grid=(B, nt),
        in_specs=[pl.BlockSpec((GLA_TILE, 2 * GLA_QK), lambda b, j: (tok(b, j), 0)),
                  pl.BlockSpec((GLA_TILE, GLA_V), lambda b, j: (tok(b, j), 1)),
                  pl.BlockSpec((GLA_TILE, GLA_V), lambda b, j: (tok(b, j), 2)),
                  pl.BlockSpec((GLA_TILE, 128), lambda b, j: (tok(b, j), GR_COLBLOCK)),
                  pl.BlockSpec((128, GLA_QK), lambda b, j: (0, 0)),
                  pl.BlockSpec((1, GLA_QK), lambda b, j: (0, 0)),
                  pl.BlockSpec((1, GLA_DV), lambda b, j: (0, 0)),
                  pl.BlockSpec((1, GLA_V), lambda b, j: (0, 0))],
        out_specs=pl.BlockSpec((GLA_TILE, GLA_V), lambda b, j: (tok(b, j), 0)),
        out_shape=jax.ShapeDtypeStruct((B * S, GLA_V), BF16),
        scratch_shapes=[pltpu.VMEM((GLA_HEADS, GLA_DV, GLA_DK), F32)],
        compiler_params=_params("arbitrary", "arbitrary"),
        name="gla",
    )(proj, proj, proj, proj, w_alpha, b_alpha, out_norm_w, mix_scale_gla)


def _norm_rope_t(xt, w_ref, cos_ref, sin_ref):
    half = MOBA_HD // 2
    outs = []
    for h in range(MOBA_HEADS):
        xh = xt[h * MOBA_HD:(h + 1) * MOBA_HD]
        ms = jnp.mean(xh * xh, axis=0, keepdims=True)
        y = xh * lax.rsqrt(ms + EPS) * w_ref[...]
        y1, y2 = y[:half], y[half:]
        c, s = cos_ref[...], sin_ref[...]
        outs.append(y1 * c - y2 * s)
        outs.append(y2 * c + y1 * s)
    return jnp.concatenate(outs, axis=0)


def _moba_prep_kernel(q_ref, k_ref, v_ref, qw_ref, kw_ref, cos_ref, sin_ref,
                      qt_ref, kh_ref, vt_ref, bias_ref, mt_ref):
    j = pl.program_id(1)
    nb = 16

    @pl.when(j == 0)
    def _():
        mt_ref[...] = jnp.zeros_like(mt_ref)

    qt = _norm_rope_t(q_ref[...].T, qw_ref, cos_ref, sin_ref)
    kt = _norm_rope_t(k_ref[...].T, kw_ref, cos_ref, sin_ref)
    qt_ref[0] = (qt * (MOBA_HD ** -0.5)).astype(BF16)
    kn = kt.T
    for h in range(MOBA_HEADS):
        kh_ref[0, h] = kn[:, h * MOBA_HD:(h + 1) * MOBA_HD].astype(BF16)
    vt_ref[0, 0] = v_ref[...].T.astype(BF16)

    bs = jnp.dot(mt_ref[...], qt, precision=HIGHEST, preferred_element_type=F32)
    n_idx = lax.broadcasted_iota(jnp.int32, (nb, MOBA_BLOCK), 0)
    for h in range(MOBA_HEADS):
        sc = jnp.where(n_idx < j, bs[h * nb:(h + 1) * nb], -jnp.inf)
        sel = jnp.zeros((nb, MOBA_BLOCK), jnp.bool_)
        for r in range(MOBA_TOPK):
            m = jnp.max(sc, axis=0, keepdims=True)
            first = jnp.min(jnp.where(sc == m, n_idx, nb), axis=0, keepdims=True)
            hit = n_idx == first
            sel = jnp.logical_or(sel, jnp.logical_and(hit, r < j))
            sc = jnp.where(hit, -jnp.inf, sc)
        bias_ref[0, 0, h] = jnp.where(sel, 0.0, NEG)

    kbar = jnp.mean(kn, axis=0, keepdims=True)
    r_idx = lax.broadcasted_iota(jnp.int32, (MOBA_HEADS * nb, MOBA_W), 0)
    c_idx = lax.broadcasted_iota(jnp.int32, (MOBA_HEADS * nb, MOBA_W), 1)
    mine = jnp.logical_and(r_idx % nb == j, r_idx // nb == c_idx // MOBA_HD)
    mt_ref[...] = jnp.where(mine, kbar, mt_ref[...])


def _moba_prep(proj, B, S, qw, kw, cos_t, sin_t):
    nb = S // MOBA_BLOCK
    assert nb <= 16
    tok = lambda b, j: b * nb + j
    half = MOBA_HD // 2
    return pl.pallas_call(
        _moba_prep_kernel,
        grid=(B, nb),
        in_specs=[pl.BlockSpec((MOBA_BLOCK, MOBA_W), lambda b, j: (tok(b, j), 3)),
                  pl.BlockSpec((MOBA_BLOCK, MOBA_W), lambda b, j: (tok(b, j), 4)),
                  pl.BlockSpec((MOBA_BLOCK, MOBA_W), lambda b, j: (tok(b, j), 5)),
                  pl.BlockSpec((MOBA_HD, MOBA_BLOCK), lambda b, j: (0, 0)),
                  pl.BlockSpec((MOBA_HD, MOBA_BLOCK), lambda b, j: (0, 0)),
                  pl.BlockSpec((half, MOBA_BLOCK), lambda b, j: (0, j)),
                  pl.BlockSpec((half, MOBA_BLOCK), lambda b, j: (0, j))],
        out_specs=[pl.BlockSpec((1, MOBA_W, MOBA_BLOCK), lambda b, j: (b, 0, j)),
                   pl.BlockSpec((1, MOBA_HEADS, MOBA_BLOCK, MOBA_HD), lambda b, j: (b, 0, j, 0)),
                   pl.BlockSpec((1, 1, MOBA_W, MOBA_BLOCK), lambda b, j: (b, j, 0, 0)),
                   pl.BlockSpec((1, 1, MOBA_HEADS, 16, MOBA_BLOCK), lambda b, j: (b, j, 0, 0, 0))],
        out_shape=[jax.ShapeDtypeStruct((B, MOBA_W, S), BF16),
                   jax.ShapeDtypeStruct((B, MOBA_HEADS, S, MOBA_HD), BF16),
                   jax.ShapeDtypeStruct((B, nb, MOBA_W, MOBA_BLOCK), BF16),
                   jax.ShapeDtypeStruct((B, nb, MOBA_HEADS, 16, MOBA_BLOCK), F32)],
        scratch_shapes=[pltpu.VMEM((MOBA_HEADS * 16, MOBA_W), F32)],
        compiler_params=_params("arbitrary", "arbitrary"),
        name="moba_prep",
    )(proj, proj, proj, qw, kw, cos_t, sin_t)


def _moba_kernel(qt_ref, k_ref, vt_ref, bias_ref, o_ref, acc_ref, m_ref, l_ref):
    j = pl.program_id(1)
    hd = MOBA_HD
    kpos = lax.broadcasted_iota(jnp.int32, (MOBA_BLOCK, MOBA_BLOCK), 0)
    qpos = lax.broadcasted_iota(jnp.int32, (MOBA_BLOCK, MOBA_BLOCK), 1)
    causal = kpos <= qpos
    own = pl.ds(pl.multiple_of(j * MOBA_BLOCK, MOBA_BLOCK), MOBA_BLOCK)
    heads = range(MOBA_HEADS)
    rows = [slice(h * hd, (h + 1) * hd) for h in heads]
    s = [jnp.where(causal, jnp.dot(k_ref[0, h, own, :], qt_ref[0, rows[h], :], preferred_element_type=F32),
                   -jnp.inf) for h in heads]
    m = [jnp.max(s[h], axis=0, keepdims=True) for h in heads]
    p = [jnp.exp(s[h] - m[h]) for h in heads]
    l = [jnp.sum(p[h], axis=0, keepdims=True) for h in heads]
    pv = [jnp.dot(vt_ref[0, j, rows[h], :], p[h].astype(BF16), preferred_element_type=F32)
          for h in heads]
    for h in heads:
        m_ref[h] = m[h]
        l_ref[h] = l[h]
        acc_ref[h] = pv[h]

    def body(n, carry):
        blk = pl.ds(pl.multiple_of(n * MOBA_BLOCK, MOBA_BLOCK), MOBA_BLOCK)
        s = [jnp.dot(k_ref[0, h, blk, :], qt_ref[0, rows[h], :], preferred_element_type=F32)
             + bias_ref[0, 0, h, pl.ds(n, 1), :] for h in heads]
        m_old = [m_ref[h] for h in heads]
        m_new = [jnp.maximum(m_old[h], jnp.max(s[h], axis=0, keepdims=True)) for h in heads]
        alpha = [jnp.exp(m_old[h] - m_new[h]) for h in heads]
        p = [jnp.exp(s[h] - m_new[h]) for h in heads]
        l_new = [alpha[h] * l_ref[h] + jnp.sum(p[h], axis=0, keepdims=True) for h in heads]
        pv = [jnp.dot(vt_ref[0, n, rows[h], :], p[h].astype(BF16), preferred_element_type=F32)
              for h in heads]
        acc_new = [acc_ref[h] * alpha[h] + pv[h] for h in heads]
        for h in heads:
            m_ref[h] = m_new[h]
            l_ref[h] = l_new[h]
            acc_ref[h] = acc_new[h]
        return carry

    lax.fori_loop(0, j, body, 0)
    for h in range(MOBA_HEADS):
        o_ref[0, h * hd:(h + 1) * hd, :] = (acc_ref[h] / l_ref[h]).astype(BF16)


def _moba(qt, kh, vt, bias, B, S):
    nb = S // MOBA_BLOCK
    return pl.pallas_call(
        _moba_kernel,
        grid=(B, nb),
        in_specs=[pl.BlockSpec((1, MOBA_W, MOBA_BLOCK), lambda b, j: (b, 0, j)),
                  pl.BlockSpec((1, MOBA_HEADS, S, MOBA_HD), lambda b, j: (b, 0, 0, 0)),
                  pl.BlockSpec((1, nb, MOBA_W, MOBA_BLOCK), lambda b, j: (b, 0, 0, 0)),
                  pl.BlockSpec((1, 1, MOBA_HEADS, 16, MOBA_BLOCK), lambda b, j: (b, j, 0, 0, 0))],
        out_specs=pl.BlockSpec((1, MOBA_W, MOBA_BLOCK), lambda b, j: (b, 0, j)),
        out_shape=jax.ShapeDtypeStruct((B, MOBA_W, S), BF16),
        scratch_shapes=[pltpu.VMEM((MOBA_HEADS, MOBA_HD, MOBA_BLOCK), F32),
                        pltpu.VMEM((MOBA_HEADS, 1, MOBA_BLOCK), F32),
                        pltpu.VMEM((MOBA_HEADS, 1, MOBA_BLOCK), F32)],
        compiler_params=_params("arbitrary", "arbitrary"),
        name="moba",
    )(qt, kh, vt, bias)


def _outproj_kernel(x_ref, og_ref, omt_ref, msm_ref, wo_ref, n2_ref, wq_ref, x1_ref, xn_ref, q_ref):
    om = (omt_ref[0].astype(F32).T * msm_ref[...]).astype(BF16)
    x1 = (x_ref[...]
          + jnp.dot(og_ref[...], wo_ref[0:GLA_V, :], preferred_element_type=F32)
          + jnp.dot(om, wo_ref[GLA_V:, :], preferred_element_type=F32))
    x1_ref[...] = x1
    ms = jnp.mean(x1 * x1, axis=-1, keepdims=True)
    xn = (x1 * lax.rsqrt(ms + EPS) * n2_ref[...]).astype(BF16)
    xn_ref[...] = xn
    q_ref[...] = jnp.dot(xn, wq_ref[...], preferred_element_type=F32).astype(BF16)


def _outproj(x2, o_gla, o_moba_t, ms_moba, w_out, norm2_w, wq, B, S):
    T = B * S
    nt = S // OUT_TILE
    QW = PEER_HEADS * PEER_QDIM
    return pl.pallas_call(
        _outproj_kernel,
        grid=(T // OUT_TILE,),
        in_specs=[pl.BlockSpec((OUT_TILE, D_MODEL), lambda i: (i, 0)),
                  pl.BlockSpec((OUT_TILE, GLA_V), lambda i: (i, 0)),
                  pl.BlockSpec((1, MOBA_W, OUT_TILE), lambda i: (i // nt, 0, i % nt)),
                  pl.BlockSpec((1, MOBA_W), lambda i: (0, 0)),
                  pl.BlockSpec((GLA_V + MOBA_W, D_MODEL), lambda i: (0, 0)),
                  pl.BlockSpec((1, D_MODEL), lambda i: (0, 0)),
                  pl.BlockSpec((D_MODEL, QW), lambda i: (0, 0))],
        out_specs=[pl.BlockSpec((OUT_TILE, D_MODEL), lambda i: (i, 0)),
                   pl.BlockSpec((OUT_TILE, D_MODEL), lambda i: (i, 0)),
                   pl.BlockSpec((OUT_TILE, QW), lambda i: (i, 0))],
        out_shape=[jax.ShapeDtypeStruct((T, D_MODEL), F32),
                   jax.ShapeDtypeStruct((T, D_MODEL), BF16),
                   jax.ShapeDtypeStruct((T, QW), BF16)],
        compiler_params=_params("arbitrary"),
        name="outproj",
    )(x2, o_gla, o_moba_t, ms_moba, w_out, norm2_w, wq)


_PAIRS = [(r, q) for r in range(PEER_TOPK + 1) for q in range(PEER_TOPK + 1)
          if (r + 1) * (q + 1) <= PEER_TOPK + 1]


def _tree(op, xs):
    xs = list(xs)
    while len(xs) > 1:
        xs = [op(xs[i], xs[i + 1]) if i + 1 < len(xs) else xs[i] for i in range(0, len(xs), 2)]
    return xs[0]


def _route_kernel(q_ref, sk_ref, rk_ref, e1_ref, cnt_ref, e0_ref, sc_ref, tmp_ref):
    H, NK = PEER_HEADS, PEER_NKEYS
    hq = H * PEER_QDIM // 2
    G = 8
    NTOP = PEER_TOPK + 1
    for p in range(2):
        sc = _nt_dot(sk_ref[p], q_ref[:, p * hq:(p + 1) * hq])
        for lt in range(ROUTE_TILE // ROUTE_LANES):
            sc_ref[p, lt] = sc[:, lt * ROUTE_LANES:(lt + 1) * ROUTE_LANES]
    neg = jnp.full((H, ROUTE_LANES), -jnp.inf, F32)

    def insert(top, v):
        out = []
        for t in top:
            out.append(jnp.maximum(t, v))
            v = jnp.minimum(t, v)
        return out

    for lt in range(ROUTE_TILE // ROUTE_LANES):
        lanes = slice(lt * ROUTE_LANES, (lt + 1) * ROUTE_LANES)

        def largest(p, lt=lt):
            def body(g, top):
                v = sc_ref[p, lt, pl.ds(pl.multiple_of(g * (G * H), G * H), G * H), :]
                top = list(top)
                for k in range(G):
                    top = insert(top, v[k * H:(k + 1) * H])
                return tuple(top)
            return lax.fori_loop(0, NK // G, body, (neg,) * NTOP)

        tops = (largest(0), largest(1))
        best = [neg] * NTOP
        for r, q in _PAIRS:
            best = insert(best, tops[0][r] + tops[1][q])
        tau = 0.5 * (best[PEER_TOPK - 1] + best[PEER_TOPK])
        top = tops[0][0] + tops[1][0]
        zinv = 0.5 / _tree(jnp.add, [jnp.where(tops[0][r] + tops[1][q] > tau,
                                               jnp.exp(tops[0][r] + tops[1][q] - top), 0.0)
                                     for r, q in _PAIRS])
        thr = [tau - b for b in tops[1]]

        def per_key(g, carry, lt=lt, tops=tops, thr=thr, zinv=zinv):
            base = pl.multiple_of(g * (G * H), G * H)
            v0 = sc_ref[0, lt, pl.ds(base, G * H), :]
            v1 = sc_ref[1, lt, pl.ds(base, G * H), :]
            for k in range(G):
                a = v0[k * H:(k + 1) * H]
                b = v1[k * H:(k + 1) * H]
                rows = pl.ds(pl.multiple_of(base + k * H, H), H)
                tmp_ref[0, rows, :] = _tree(jnp.add, [jnp.where(a > t, 1.0, 0.0) for t in thr])
                tmp_ref[1, rows, :] = jnp.exp(a - tops[0][0]) * zinv
                tmp_ref[2, rows, :] = _tree(jnp.add, [jnp.where(b <= t, 1.0, 0.0) for t in tops[1]])
                tmp_ref[3, rows, :] = jnp.exp(b - tops[1][0])
            return carry

        lax.fori_loop(0, NK // G, per_key, 0)
        for h in range(H):
            cnt_ref[h, :, lanes] = tmp_ref[0, pl.ds(h, NK, stride=H), :]
            e0_ref[h, :, lanes] = tmp_ref[1, pl.ds(h, NK, stride=H), :]
            rk_ref[h, :, lanes] = tmp_ref[2, pl.ds(h, NK, stride=H), :].astype(BF16)
            e1_ref[h, :, lanes] = tmp_ref[3, pl.ds(h, NK, stride=H), :].astype(BF16)


def _route(q, skbig):
    T = q.shape[0]
    H, NK = PEER_HEADS, PEER_NKEYS
    hq = H * PEER_QDIM // 2
    spec = pl.BlockSpec((H, NK, ROUTE_TILE), lambda i: (0, 0, i))
    return pl.pallas_call(
        _route_kernel,
        grid=(T // ROUTE_TILE,),
        in_specs=[pl.BlockSpec((ROUTE_TILE, 2 * hq), lambda i: (i, 0)),
                  pl.BlockSpec((2, NK * H, hq), lambda i: (0, 0, 0))],
        out_specs=[spec, spec, spec, spec],
        out_shape=[jax.ShapeDtypeStruct((H, NK, T), BF16), jax.ShapeDtypeStruct((H, NK, T), BF16),
                   jax.ShapeDtypeStruct((H, NK, T), F32), jax.ShapeDtypeStruct((H, NK, T), F32)],
        scratch_shapes=[pltpu.VMEM((2, ROUTE_TILE // ROUTE_LANES, NK * H, ROUTE_LANES), F32),
                        pltpu.VMEM((4, NK * H, ROUTE_LANES), F32)],
        compiler_params=_params("arbitrary"),
        name="peer_route",
    )(q, skbig)


PEER_NCHUNK = PEER_N // PEER_ECHUNK


def _peer_kernel(xn_ref, x1_ref, u_ref, vt_ref, rk_ref, e1_ref, cnt_ref, e0_ref, o_ref,
                 acc_ref, act_ref, w_ref):
    c = pl.program_id(1)
    NK = PEER_NKEYS
    nsub = PEER_ECHUNK // NK
    RB = PEER_STRIP_ROWS

    @pl.when(c == 0)
    def _():
        acc_ref[...] = jnp.zeros_like(acc_ref)

    act_ref[...] = _nt_dot(u_ref[...], xn_ref[...])
    for ii in range(nsub):
        i = c * nsub + ii
        cnt_rows = [cnt_ref[h, pl.ds(i, 1), :] for h in range(PEER_HEADS)]
        e0_rows = [e0_ref[h, pl.ds(i, 1), :] for h in range(PEER_HEADS)]
        for lc in range(PEER_TILE // LANES):
            lanes = slice(lc * LANES, (lc + 1) * LANES)
            cnts = [jnp.broadcast_to(r[:, lanes], (RB, LANES)).astype(BF16) for r in cnt_rows]
            e0s = [jnp.broadcast_to(r[:, lanes], (RB, LANES)).astype(BF16) for r in e0_rows]
            for rb in range(NK // RB):
                rows = slice(rb * RB, (rb + 1) * RB)
                wsum = _tree(jnp.add, [
                    e0s[h] * jnp.where(rk_ref[h, rows, lanes] <= cnts[h], e1_ref[h, rows, lanes], 0.0)
                    for h in range(PEER_HEADS)])
                er = slice(ii * NK + rb * RB, ii * NK + (rb + 1) * RB)
                a = act_ref[er, lanes]
                g2 = a * (1.0 + lax.erf(a * (1.0 / math.sqrt(2.0))))
                w_ref[er, lanes] = g2.astype(BF16) * wsum
    acc_ref[...] += jnp.dot(vt_ref[0], w_ref[...], preferred_element_type=F32)

    @pl.when(c == PEER_NCHUNK - 1)
    def _():
        o_ref[...] = x1_ref[...] + acc_ref[...].T


def _peer(xn, x1, u, vt, rk, e1, cnt, e0):
    T = xn.shape[0]
    H, NK = PEER_HEADS, PEER_NKEYS
    rspec = pl.BlockSpec((H, NK, PEER_TILE), lambda t, c: (0, 0, t))
    return pl.pallas_call(
        _peer_kernel,
        grid=(T // PEER_TILE, PEER_NCHUNK),
        in_specs=[pl.BlockSpec((PEER_TILE, D_MODEL), lambda t, c: (t, 0)),
                  pl.BlockSpec((PEER_TILE, D_MODEL), lambda t, c: (t, 0)),
                  pl.BlockSpec((PEER_ECHUNK, D_MODEL), lambda t, c: (c, 0)),
                  pl.BlockSpec((1, D_MODEL, PEER_ECHUNK), lambda t, c: (c, 0, 0)),
                  rspec, rspec, rspec, rspec],
        out_specs=pl.BlockSpec((PEER_TILE, D_MODEL), lambda t, c: (t, 0)),
        out_shape=jax.ShapeDtypeStruct((T, D_MODEL), F32),
        scratch_shapes=[pltpu.VMEM((D_MODEL, PEER_TILE), F32),
                        pltpu.VMEM((PEER_ECHUNK, PEER_TILE), F32),
                        pltpu.VMEM((PEER_ECHUNK, PEER_TILE), BF16)],
        compiler_params=_params("arbitrary", "arbitrary"),
        name="peer",
    )(xn, x1, u, vt, rk, e1, cnt, e0)


def _layer(x, norm1_w, w_in, w_alpha, b_alpha, out_norm_w, qn_w, kn_w, mix_scale, w_out,
           norm2_w, w_query, subkeys, u_tab, v_tab):
    B, S, D = x.shape
    T = B * S
    x2 = x.reshape(T, D)

    o_gr = 2 * GLA_QK + 2 * GLA_V
    w_r = jnp.concatenate([w_in[:, :o_gr], w_in[:, o_gr + GLA_RANK:], w_in[:, o_gr:o_gr + GLA_RANK],
                           jnp.zeros((D, 128 - GLA_RANK), w_in.dtype)], axis=1).astype(BF16)
    wa = jnp.concatenate([w_alpha, jnp.zeros((128 - GLA_RANK, GLA_QK), w_alpha.dtype)], axis=0)

    proj = _inproj(x2, norm1_w.reshape(1, D), w_r)
    o_gla = _gla(proj, B, S, wa, b_alpha.reshape(1, GLA_QK), out_norm_w.reshape(1, GLA_DV),
                 mix_scale[:GLA_V].reshape(1, GLA_V))

    half = MOBA_HD // 2
    inv = ROPE_THETA ** (-jnp.arange(half, dtype=F32) / half)
    ang = inv[:, None] * jnp.arange(S, dtype=F32)[None, :]
    qw = jnp.broadcast_to(qn_w.reshape(MOBA_HD, 1), (MOBA_HD, MOBA_BLOCK))
    kw = jnp.broadcast_to(kn_w.reshape(MOBA_HD, 1), (MOBA_HD, MOBA_BLOCK))
    qt, kh, vt, bias = _moba_prep(proj, B, S, qw, kw, jnp.cos(ang), jnp.sin(ang))
    o_moba_t = _moba(qt, kh, vt, bias, B, S)

    H, NK, hd = PEER_HEADS, PEER_NKEYS, PEER_QDIM // 2
    wq = w_query.reshape(D, H, 2, hd).transpose(0, 2, 1, 3).reshape(D, 2 * H * hd).astype(BF16)
    eye = jnp.eye(H, dtype=subkeys.dtype)
    skbig = jnp.einsum('hpkd,hg->pkhgd', subkeys, eye).reshape(2, NK * H, H * hd).astype(BF16)

    x1, xn, q = _outproj(x2, o_gla, o_moba_t, mix_scale[GLA_V:].reshape(1, MOBA_W),
                         w_out.astype(BF16), norm2_w.reshape(1, D), wq, B, S)
    rk, e1, cnt, e0 = _route(q, skbig)
    vt = v_tab.reshape(PEER_NCHUNK, PEER_ECHUNK, D).transpose(0, 2, 1).astype(BF16)
    out = _peer(xn, x1, u_tab.astype(BF16), vt, rk, e1, cnt, e0)
    return out.reshape(B, S, D)


def kernel(x, norm1_w, w_in, gla_w_alpha, gla_b_alpha, gla_out_norm_w, moba_q_norm_w, moba_k_norm_w,
           mix_scale, w_out, norm2_w, peer_w_query, peer_subkeys, peer_u, peer_v):
    assert norm1_w.shape[0] == 1, "single-layer kernel"
    return _layer(x, norm1_w[0], w_in[0], gla_w_alpha[0], gla_b_alpha[0], gla_out_norm_w[0],
                  moba_q_norm_w[0], moba_k_norm_w[0], mix_scale[0], w_out[0], norm2_w[0],
                  peer_w_query[0], peer_subkeys[0], peer_u[0], peer_v[0])
```

```python
import math

import jax
import jax.numpy as jnp
from jax import lax
from jax.experimental import pallas as pl
from jax.experimental.pallas import tpu as pltpu

D_MODEL = 1024
GLA_HEADS, GLA_DK, GLA_DV, GLA_RANK, GLA_TAU, GLA_CHUNK = 4, 64, 128, 16, 16.0, 64
MOBA_HEADS, MOBA_HD, MOBA_BLOCK, MOBA_TOPK = 8, 64, 256, 3
ROPE_THETA = 10000.0
PEER_HEADS, PEER_NKEYS, PEER_QDIM, PEER_TOPK = 8, 128, 256, 16
PEER_N = PEER_NKEYS * PEER_NKEYS
EPS = 1e-6

GLA_QK = GLA_HEADS * GLA_DK
GLA_V = GLA_HEADS * GLA_DV
MOBA_W = MOBA_HEADS * MOBA_HD
PROJ_W = 2 * GLA_QK + 2 * GLA_V + 3 * MOBA_W + 128
GR_COLBLOCK = (PROJ_W - 128) // 128

F32 = jnp.float32
BF16 = jnp.bfloat16
HIGHEST = lax.Precision.HIGHEST
NEG = -1e30
LOG2E = math.log2(math.e)

VMEM_LIMIT_BYTES = 56 * 1024 * 1024

IN_TILE = 512
GLA_TILE = 256
OUT_TILE = 256
ROUTE_TILE = 256
ROUTE_LANES = 128
PEER_TILE = 512
PEER_ECHUNK = 2048
PEER_ACT_PIECES = 2
PEER_STRIP_ROWS = 32
LANES = 256


def _params(*sem):
    return pltpu.CompilerParams(dimension_semantics=sem, vmem_limit_bytes=VMEM_LIMIT_BYTES)


def _nt_dot(a, b, precision=None):
    return lax.dot_general(a, b, (((1,), (1,)), ((), ())), precision=precision,
                           preferred_element_type=F32)


def _inproj_kernel(x_ref, nw_ref, w_ref, o_ref):
    x = x_ref[...]
    ms = jnp.mean(x * x, axis=-1, keepdims=True)
    xn = (x * lax.rsqrt(ms + EPS) * nw_ref[...]).astype(BF16)
    o_ref[...] = jnp.dot(xn, w_ref[...], preferred_element_type=F32)


def _inproj(x2, norm_w, w):
    T = x2.shape[0]
    return pl.pallas_call(
        _inproj_kernel,
        grid=(T // IN_TILE,),
        in_specs=[pl.BlockSpec((IN_TILE, D_MODEL), lambda i: (i, 0)),
                  pl.BlockSpec((1, D_MODEL), lambda i: (0, 0)),
                  pl.BlockSpec((D_MODEL, PROJ_W), lambda i: (0, 0))],
        out_specs=pl.BlockSpec((IN_TILE, PROJ_W), lambda i: (i, 0)),
        out_shape=jax.ShapeDtypeStruct((T, PROJ_W), F32),
        compiler_params=_params("arbitrary"),
        name="inproj",
    )(x2, norm_w, w)


def _gla_kernel(qk_ref, v_ref, gate_ref, gr_ref, wa_ref, ba_ref, nw_ref, ms_ref, o_ref, st_ref):
    @pl.when(pl.program_id(1) == 0)
    def _():
        st_ref[...] = jnp.zeros_like(st_ref)

    C = GLA_CHUNK
    z = jnp.dot(gr_ref[...], wa_ref[...], precision=HIGHEST, preferred_element_type=F32) + ba_ref[...]
    log_a = (jnp.minimum(z, 0.0) - jnp.log1p(jnp.exp(-jnp.abs(z)))) * (1.0 / GLA_TAU)
    row = lax.broadcasted_iota(jnp.int32, (C, C), 0)
    col = lax.broadcasted_iota(jnp.int32, (C, C), 1)
    causal = col <= row
    tril = causal.astype(F32)
    chunks = range(GLA_TILE // C)
    heads = range(GLA_HEADS)
    ks = [slice(h * GLA_DK, (h + 1) * GLA_DK) for h in heads]
    vs = [slice(h * GLA_DV, (h + 1) * GLA_DV) for h in heads]
    sl = [slice(c * C, (c + 1) * C) for c in chunks]
    q_d, k_in, k_st, decay = [], [], [], []
    for c in chunks:
        b = jnp.dot(tril, log_a[sl[c]], precision=HIGHEST, preferred_element_type=F32)
        b_last = b[C - 1:C, :]
        q = qk_ref[sl[c], 0:GLA_QK]
        k = qk_ref[sl[c], GLA_QK:2 * GLA_QK]
        q_d.append(q * jnp.exp(b) * (GLA_DK ** -0.5))
        k_in.append(k * jnp.exp(-b))
        k_st.append((k * jnp.exp(b_last - b)).astype(BF16))
        decay.append(jnp.exp(b_last))
    v = [[v_ref[sl[c], vs[h]] for h in heads] for c in chunks]
    a = [[jnp.where(causal, _nt_dot(q_d[c][:, ks[h]], k_in[c][:, ks[h]], precision=HIGHEST), 0.0)
          for h in heads] for c in chunks]
    o = [[jnp.dot(a[c][h].astype(BF16), v[c][h].astype(BF16), preferred_element_type=F32)
          for h in heads] for c in chunks]
    upd = [[jnp.dot(v[c][h].T.astype(BF16), k_st[c][:, ks[h]], preferred_element_type=F32)
            for h in heads] for c in chunks]
    st = [st_ref[h] for h in heads]
    for c in chunks:
        for h in heads:
            o[c][h] = o[c][h] + _nt_dot(q_d[c][:, ks[h]].astype(BF16), st[h].astype(BF16))
            st[h] = st[h] * decay[c][:, ks[h]] + upd[c][h]
    out = []
    for c in chunks:
        for h in heads:
            y = o[c][h]
            y = y * lax.rsqrt(jnp.mean(y * y, axis=-1, keepdims=True) + EPS) * nw_ref[...]
            g = gate_ref[sl[c], vs[h]]
            out.append((y * (g * jax.nn.sigmoid(g)) * ms_ref[:, vs[h]]).astype(BF16))
    for c in chunks:
        for h in heads:
            o_ref[sl[c], vs[h]] = out[c * GLA_HEADS + h]
    for h in heads:
        st_ref[h] = st[h]


def _gla(proj, B, S, w_alpha, b_alpha, out_norm_w, mix_scale_gla):
    nt = S // GLA_TILE
    tok = lambda b, j: b * nt + j
    return pl.pallas_call(
        _gla_kernel,
        grid=(B, nt),
        in_specs=[pl.BlockSpec((GLA_TILE, 2 * GLA_QK), lambda b, j: (tok(b, j), 0)),
                  pl.BlockSpec((GLA_TILE, GLA_V), lambda b, j: (tok(b, j), 1)),
                  pl.BlockSpec((GLA_TILE, GLA_V), lambda b, j: (tok(b, j), 2)),
                  pl.BlockSpec((GLA_TILE, 128), lambda b, j: (tok(b, j), GR_COLBLOCK)),
                  pl.BlockSpec((128, GLA_QK), lambda b, j: (0, 0)),
                  pl.BlockSpec((1, GLA_QK), lambda b, j: (0, 0)),
                  pl.BlockSpec((1, GLA_DV), lambda b, j: (0, 0)),
                  pl.BlockSpec((1, GLA_V), lambda b, j: (0, 0))],
        out_specs=pl.BlockSpec((GLA_TILE, GLA_V), lambda b, j: (tok(b, j), 0)),
        out_shape=jax.ShapeDtypeStruct((B * S, GLA_V), BF16),
        scratch_shapes=[pltpu.VMEM((GLA_HEADS, GLA_DV, GLA_DK), F32)],
        compiler_params=_params("arbitrary", "arbitrary"),
        name="gla",
    )(proj, proj, proj, proj, w_alpha, b_alpha, out_norm_w, mix_scale_gla)


def _norm_rope_t(xt, w_ref, cos_ref, sin_ref):
    half = MOBA_HD // 2
    outs = []
    for h in range(MOBA_HEADS):
        xh = xt[h * MOBA_HD:(h + 1) * MOBA_HD]
        ms = jnp.mean(xh * xh, axis=0, keepdims=True)
        y = xh * lax.rsqrt(ms + EPS) * w_ref[...]
        y1, y2 = y[:half], y[half:]
        c, s = cos_ref[...], sin_ref[...]
        outs.append(y1 * c - y2 * s)
        outs.append(y2 * c + y1 * s)
    return jnp.concatenate(outs, axis=0)


def _moba_prep_kernel(q_ref, k_ref, v_ref, qw_ref, kw_ref, cos_ref, sin_ref,
                      qt_ref, kh_ref, vt_ref, bias_ref, mt_ref):
    j = pl.program_id(1)
    nb = 16

    @pl.when(j == 0)
    def _():
        mt_ref[...] = jnp.zeros_like(mt_ref)

    qt = _norm_rope_t(q_ref[...].T, qw_ref, cos_ref, sin_ref)
    kt = _norm_rope_t(k_ref[...].T, kw_ref, cos_ref, sin_ref)
    qt_ref[0] = (qt * (MOBA_HD ** -0.5 * LOG2E)).astype(BF16)
    kn = kt.T
    for h in range(MOBA_HEADS):
        kh_ref[0, h] = kn[:, h * MOBA_HD:(h + 1) * MOBA_HD].astype(BF16)
    vt_ref[0, 0] = v_ref[...].T.astype(BF16)

    bs = jnp.dot(mt_ref[...], qt, precision=HIGHEST, preferred_element_type=F32)
    n_idx = lax.broadcasted_iota(jnp.int32, (nb, MOBA_BLOCK), 0)
    for h in range(MOBA_HEADS):
        sc = jnp.where(n_idx < j, bs[h * nb:(h + 1) * nb], -jnp.inf)
        sel = jnp.zeros((nb, MOBA_BLOCK), jnp.bool_)
        for r in range(MOBA_TOPK):
            m = jnp.max(sc, axis=0, keepdims=True)
            first = jnp.min(jnp.where(sc == m, n_idx, nb), axis=0, keepdims=True)
            hit = n_idx == first
            sel = jnp.logical_or(sel, jnp.logical_and(hit, r < j))
            sc = jnp.where(hit, -jnp.inf, sc)
        bias_ref[0, 0, h] = jnp.where(sel, 0.0, NEG)

    kbar = jnp.mean(kn, axis=0, keepdims=True)
    r_idx = lax.broadcasted_iota(jnp.int32, (MOBA_HEADS * nb, MOBA_W), 0)
    c_idx = lax.broadcasted_iota(jnp.int32, (MOBA_HEADS * nb, MOBA_W), 1)
    mine = jnp.logical_and(r_idx % nb == j, r_idx // nb == c_idx // MOBA_HD)
    mt_ref[...] = jnp.where(mine, kbar, mt_ref[...])


def _moba_prep(proj, B, S, qw, kw, cos_t, sin_t):
    nb = S // MOBA_BLOCK
    assert nb <= 16
    tok = lambda b, j: b * nb + j
    half = MOBA_HD // 2
    return pl.pallas_call(
        _moba_prep_kernel,
        grid=(B, nb),
        in_specs=[pl.BlockSpec((MOBA_BLOCK, MOBA_W), lambda b, j: (tok(b, j), 3)),
                  pl.BlockSpec((MOBA_BLOCK, MOBA_W), lambda b, j: (tok(b, j), 4)),
                  pl.BlockSpec((MOBA_BLOCK, MOBA_W), lambda b, j: (tok(b, j), 5)),
                  pl.BlockSpec((MOBA_HD, MOBA_BLOCK), lambda b, j: (0, 0)),
                  pl.BlockSpec((MOBA_HD, MOBA_BLOCK), lambda b, j: (0, 0)),
                  pl.BlockSpec((half, MOBA_BLOCK), lambda b, j: (0, j)),
                  pl.BlockSpec((half, MOBA_BLOCK), lambda b, j: (0, j))],
        out_specs=[pl.BlockSpec((1, MOBA_W, MOBA_BLOCK), lambda b, j: (b, 0, j)),
                   pl.BlockSpec((1, MOBA_HEADS, MOBA_BLOCK, MOBA_HD), lambda b, j: (b, 0, j, 0)),
                   pl.BlockSpec((1, 1, MOBA_W, MOBA_BLOCK), lambda b, j: (b, j, 0, 0)),
                   pl.BlockSpec((1, 1, MOBA_HEADS, 16, MOBA_BLOCK), lambda b, j: (b, j, 0, 0, 0))],
        out_shape=[jax.ShapeDtypeStruct((B, MOBA_W, S), BF16),
                   jax.ShapeDtypeStruct((B, MOBA_HEADS, S, MOBA_HD), BF16),
                   jax.ShapeDtypeStruct((B, nb, MOBA_W, MOBA_BLOCK), BF16),
                   jax.ShapeDtypeStruct((B, nb, MOBA_HEADS, 16, MOBA_BLOCK), F32)],
        scratch_shapes=[pltpu.VMEM((MOBA_HEADS * 16, MOBA_W), F32)],
        compiler_params=_params("arbitrary", "arbitrary"),
        name="moba_prep",
    )(proj, proj, proj, qw, kw, cos_t, sin_t)


def _moba_kernel(qt_ref, k_ref, vt_ref, bias_ref, o_ref, acc_ref, m_ref, l_ref):
    j = pl.program_id(1)
    hd = MOBA_HD
    kpos = lax.broadcasted_iota(jnp.int32, (MOBA_BLOCK, MOBA_BLOCK), 0)
    qpos = lax.broadcasted_iota(jnp.int32, (MOBA_BLOCK, MOBA_BLOCK), 1)
    causal = kpos <= qpos
    own = pl.ds(pl.multiple_of(j * MOBA_BLOCK, MOBA_BLOCK), MOBA_BLOCK)
    heads = range(MOBA_HEADS)
    rows = [slice(h * hd, (h + 1) * hd) for h in heads]
    s = [jnp.where(causal, jnp.dot(k_ref[0, h, own, :], qt_ref[0, rows[h], :], preferred_element_type=F32),
                   -jnp.inf) for h in heads]
    m = [jnp.max(s[h], axis=0, keepdims=True) for h in heads]
    p = [jnp.exp2(s[h] - m[h]) for h in heads]
    l = [jnp.sum(p[h], axis=0, keepdims=True) for h in heads]
    pv = [jnp.dot(vt_ref[0, j, rows[h], :], p[h].astype(BF16), preferred_element_type=F32)
          for h in heads]
    for h in heads:
        m_ref[h] = m[h]
        l_ref[h] = l[h]
        acc_ref[h] = pv[h]

    def body(n, carry):
        blk = pl.ds(pl.multiple_of(n * MOBA_BLOCK, MOBA_BLOCK), MOBA_BLOCK)
        s = {h: jnp.dot(k_ref[0, h, blk, :], qt_ref[0, rows[h], :], preferred_element_type=F32)
             + bias_ref[0, 0, h, pl.ds(n, 1), :] for h in heads}
        m_old = {h: m_ref[h] for h in heads}
        m_new = {h: jnp.maximum(m_old[h], jnp.max(s[h], axis=0, keepdims=True)) for h in heads}
        alpha = {h: jnp.exp2(m_old[h] - m_new[h]) for h in heads}
        p = {h: jnp.exp2(s[h] - m_new[h]) for h in heads}
        l_new = {h: alpha[h] * l_ref[h] + jnp.sum(p[h], axis=0, keepdims=True) for h in heads}
        pv = {h: jnp.dot(vt_ref[0, n, rows[h], :], p[h].astype(BF16), preferred_element_type=F32)
              for h in heads}
        acc_new = {h: acc_ref[h] * alpha[h] + pv[h] for h in heads}
        for h in heads:
            m_ref[h] = m_new[h]
            l_ref[h] = l_new[h]
            acc_ref[h] = acc_new[h]
        return carry

    lax.fori_loop(0, j, body, 0)
    for h in range(MOBA_HEADS):
        o_ref[0, h * hd:(h + 1) * hd, :] = (acc_ref[h] / l_ref[h]).astype(BF16)


def _moba(qt, kh, vt, bias, B, S):
    nb = S // MOBA_BLOCK
    return pl.pallas_call(
        _moba_kernel,
        grid=(B, nb),
        in_specs=[pl.BlockSpec((1, MOBA_W, MOBA_BLOCK), lambda b, j: (b, 0, j)),
                  pl.BlockSpec((1, MOBA_HEADS, S, MOBA_HD), lambda b, j: (b, 0, 0, 0)),
                  pl.BlockSpec((1, nb, MOBA_W, MOBA_BLOCK), lambda b, j: (b, 0, 0, 0)),
                  pl.BlockSpec((1, 1, MOBA_HEADS, 16, MOBA_BLOCK), lambda b, j: (b, j, 0, 0, 0))],
        out_specs=pl.BlockSpec((1, MOBA_W, MOBA_BLOCK), lambda b, j: (b, 0, j)),
        out_shape=jax.ShapeDtypeStruct((B, MOBA_W, S), BF16),
        scratch_shapes=[pltpu.VMEM((MOBA_HEADS, MOBA_HD, MOBA_BLOCK), F32),
                        pltpu.VMEM((MOBA_HEADS, 1, MOBA_BLOCK), F32),
                        pltpu.VMEM((MOBA_HEADS, 1, MOBA_BLOCK), F32)],
        compiler_params=_params("arbitrary", "arbitrary"),
        name="moba",
    )(qt, kh, vt, bias)


def _outproj_kernel(x_ref, og_ref, omt_ref, msm_ref, wo_ref, n2_ref, wq_ref, x1_ref, xn_ref, q_ref):
    om = (omt_ref[0].astype(F32).T * msm_ref[...]).astype(BF16)
    x1 = (x_ref[...]
          + jnp.dot(og_ref[...], wo_ref[0:GLA_V, :], preferred_element_type=F32)
          + jnp.dot(om, wo_ref[GLA_V:, :], preferred_element_type=F32))
    x1_ref[...] = x1
    ms = jnp.mean(x1 * x1, axis=-1, keepdims=True)
    xn = (x1 * lax.rsqrt(ms + EPS) * n2_ref[...]).astype(BF16)
    xn_ref[...] = xn
    q_ref[...] = jnp.dot(xn, wq_ref[...], preferred_element_type=F32).astype(BF16)


def _outproj(x2, o_gla, o_moba_t, ms_moba, w_out, norm2_w, wq, B, S):
    T = B * S
    nt = S // OUT_TILE
    QW = PEER_HEADS * PEER_QDIM
    return pl.pallas_call(
        _outproj_kernel,
        grid=(T // OUT_TILE,),
        in_specs=[pl.BlockSpec((OUT_TILE, D_MODEL), lambda i: (i, 0)),
                  pl.BlockSpec((OUT_TILE, GLA_V), lambda i: (i, 0)),
                  pl.BlockSpec((1, MOBA_W, OUT_TILE), lambda i: (i // nt, 0, i % nt)),
                  pl.BlockSpec((1, MOBA_W), lambda i: (0, 0)),
                  pl.BlockSpec((GLA_V + MOBA_W, D_MODEL), lambda i: (0, 0)),
                  pl.BlockSpec((1, D_MODEL), lambda i: (0, 0)),
                  pl.BlockSpec((D_MODEL, QW), lambda i: (0, 0))],
        out_specs=[pl.BlockSpec((OUT_TILE, D_MODEL), lambda i: (i, 0)),
                   pl.BlockSpec((OUT_TILE, D_MODEL), lambda i: (i, 0)),
                   pl.BlockSpec((OUT_TILE, QW), lambda i: (i, 0))],
        out_shape=[jax.ShapeDtypeStruct((T, D_MODEL), F32),
                   jax.ShapeDtypeStruct((T, D_MODEL), BF16),
                   jax.ShapeDtypeStruct((T, QW), BF16)],
        compiler_params=_params("arbitrary"),
        name="outproj",
    )(x2, o_gla, o_moba_t, ms_moba, w_out, norm2_w, wq)


_PAIRS = [(r, q) for r in range(PEER_TOPK + 1) for q in range(PEER_TOPK + 1)
          if (r + 1) * (q + 1) <= PEER_TOPK + 1]


def _tree(op, xs):
    xs = list(xs)
    while len(xs) > 1:
        xs = [op(xs[i], xs[i + 1]) if i + 1 < len(xs) else xs[i] for i in range(0, len(xs), 2)]
    return xs[0]


def _route_kernel(q_ref, sk_ref, rk_ref, e1_ref, cnt_ref, e0_ref, sc_ref, tmp_ref):
    H, NK = PEER_HEADS, PEER_NKEYS
    hq = H * PEER_QDIM // 2
    G = 8
    NTOP = PEER_TOPK + 1
    for p in range(2):
        sc = _nt_dot(sk_ref[p], q_ref[:, p * hq:(p + 1) * hq])
        for lt in range(ROUTE_TILE // ROUTE_LANES):
            sc_ref[p, lt] = sc[:, lt * ROUTE_LANES:(lt + 1) * ROUTE_LANES]
    neg = jnp.full((H, ROUTE_LANES), -jnp.inf, F32)

    def insert(top, v):
        out = []
        for t in top:
            out.append(jnp.maximum(t, v))
            v = jnp.minimum(t, v)
        return out

    for lt in range(ROUTE_TILE // ROUTE_LANES):
        lanes = slice(lt * ROUTE_LANES, (lt + 1) * ROUTE_LANES)

        def largest(p, lt=lt):
            def body(g, top):
                v = sc_ref[p, lt, pl.ds(pl.multiple_of(g * (G * H), G * H), G * H), :]
                top = list(top)
                for k in range(G):
                    top = insert(top, v[k * H:(k + 1) * H])
                return tuple(top)
            return lax.fori_loop(0, NK // G, body, (neg,) * NTOP)

        tops = (largest(0), largest(1))
        best = [neg] * NTOP
        for r, q in _PAIRS:
            best = insert(best, tops[0][r] + tops[1][q])
        tau = 0.5 * (best[PEER_TOPK - 1] + best[PEER_TOPK])
        top = tops[0][0] + tops[1][0]
        zinv = 0.5 / _tree(jnp.add, [jnp.where(tops[0][r] + tops[1][q] > tau,
                                               jnp.exp(tops[0][r] + tops[1][q] - top), 0.0)
                                     for r, q in _PAIRS])
        thr = [tau - b for b in tops[1]]

        def per_key(g, carry, lt=lt, tops=tops, thr=thr, zinv=zinv):
            base = pl.multiple_of(g * (G * H), G * H)
            v0 = sc_ref[0, lt, pl.ds(base, G * H), :]
            v1 = sc_ref[1, lt, pl.ds(base, G * H), :]
            for k in range(G):
                a = v0[k * H:(k + 1) * H]
                b = v1[k * H:(k + 1) * H]
                rows = pl.ds(pl.multiple_of(base + k * H, H), H)
                tmp_ref[0, rows, :] = _tree(jnp.add, [jnp.where(a > t, 1.0, 0.0) for t in thr])
                tmp_ref[1, rows, :] = jnp.exp(a - tops[0][0]) * zinv
                tmp_ref[2, rows, :] = _tree(jnp.add, [jnp.where(b <= t, 1.0, 0.0) for t in tops[1]])
                tmp_ref[3, rows, :] = jnp.exp(b - tops[1][0])
            return carry

        lax.fori_loop(0, NK // G, per_key, 0)
        for h in range(H):
            cnt_ref[h, :, lanes] = tmp_ref[0, pl.ds(h, NK, stride=H), :]
            e0_ref[h, :, lanes] = tmp_ref[1, pl.ds(h, NK, stride=H), :]
            rk_ref[h, :, lanes] = tmp_ref[2, pl.ds(h, NK, stride=H), :].astype(BF16)
            e1_ref[h, :, lanes] = tmp_ref[3, pl.ds(h, NK, stride=H), :].astype(BF16)


def _route(q, skbig):
    T = q.shape[0]
    H, NK = PEER_HEADS, PEER_NKEYS
    hq = H * PEER_QDIM // 2
    spec = pl.BlockSpec((H, NK, ROUTE_TILE), lambda i: (0, 0, i))
    return pl.pallas_call(
        _route_kernel,
        grid=(T // ROUTE_TILE,),
        in_specs=[pl.BlockSpec((ROUTE_TILE, 2 * hq), lambda i: (i, 0)),
                  pl.BlockSpec((2, NK * H, hq), lambda i: (0, 0, 0))],
        out_specs=[spec, spec, spec, spec],
        out_shape=[jax.ShapeDtypeStruct((H, NK, T), BF16), jax.ShapeDtypeStruct((H, NK, T), BF16),
                   jax.ShapeDtypeStruct((H, NK, T), F32), jax.ShapeDtypeStruct((H, NK, T), F32)],
        scratch_shapes=[pltpu.VMEM((2, ROUTE_TILE // ROUTE_LANES, NK * H, ROUTE_LANES), F32),
                        pltpu.VMEM((4, NK * H, ROUTE_LANES), F32)],
        compiler_params=_params("arbitrary"),
        name="peer_route",
    )(q, skbig)


PEER_NCHUNK = PEER_N // PEER_ECHUNK


def _peer_kernel(xn_ref, x1_ref, u_ref, vt_ref, rk_ref, e1_ref, cnt_ref, e0_ref, o_ref,
                 acc_ref, act_ref, w_ref):
    c = pl.program_id(1)
    NK = PEER_NKEYS
    nsub = PEER_ECHUNK // NK
    RB = PEER_STRIP_ROWS

    @pl.when(c == 0)
    def _():
        acc_ref[...] = jnp.zeros_like(acc_ref)

    piece = PEER_ECHUNK // PEER_ACT_PIECES
    for ii in range(nsub):
        if (ii * NK) % piece == 0:
            hs = slice(ii * NK, ii * NK + piece)
            act_ref[hs, :] = _nt_dot(u_ref[hs, :], xn_ref[...])
        i = c * nsub + ii
        cnt_rows = [cnt_ref[h, pl.ds(i, 1), :] for h in range(PEER_HEADS)]
        e0_rows = [e0_ref[h, pl.ds(i, 1), :] for h in range(PEER_HEADS)]
        for lc in range(PEER_TILE // LANES):
            lanes = slice(lc * LANES, (lc + 1) * LANES)
            cnts = [jnp.broadcast_to(r[:, lanes], (RB, LANES)).astype(BF16) for r in cnt_rows]
            e0s = [jnp.broadcast_to(r[:, lanes], (RB, LANES)).astype(BF16) for r in e0_rows]
            for rb in range(NK // RB):
                rows = slice(rb * RB, (rb + 1) * RB)
                wsum = _tree(jnp.add, [
                    e0s[h] * jnp.where(rk_ref[h, rows, lanes] <= cnts[h], e1_ref[h, rows, lanes], 0.0)
                    for h in range(PEER_HEADS)])
                er = slice(ii * NK + rb * RB, ii * NK + (rb + 1) * RB)
                a = act_ref[er, lanes]
                g2 = a * (1.0 + lax.erf(a * (1.0 / math.sqrt(2.0))))
                w_ref[er, lanes] = g2.astype(BF16) * wsum
    acc_ref[...] += jnp.dot(vt_ref[0], w_ref[...], preferred_element_type=F32)

    @pl.when(c == PEER_NCHUNK - 1)
    def _():
        o_ref[...] = x1_ref[...] + acc_ref[...].T


def _peer(xn, x1, u, vt, rk, e1, cnt, e0):
    T = xn.shape[0]
    H, NK = PEER_HEADS, PEER_NKEYS
    rspec = pl.BlockSpec((H, NK, PEER_TILE), lambda t, c: (0, 0, t))
    return pl.pallas_call(
        _peer_kernel,
        grid=(T // PEER_TILE, PEER_NCHUNK),
        in_specs=[pl.BlockSpec((PEER_TILE, D_MODEL), lambda t, c: (t, 0)),
                  pl.BlockSpec((PEER_TILE, D_MODEL), lambda t, c: (t, 0)),
                  pl.BlockSpec((PEER_ECHUNK, D_MODEL), lambda t, c: (c, 0)),
                  pl.BlockSpec((1, D_MODEL, PEER_ECHUNK), lambda t, c: (c, 0, 0)),
                  rspec, rspec, rspec, rspec],
        out_specs=pl.BlockSpec((PEER_TILE, D_MODEL), lambda t, c: (t, 0)),
        out_shape=jax.ShapeDtypeStruct((T, D_MODEL), F32),
        scratch_shapes=[pltpu.VMEM((D_MODEL, PEER_TILE), F32),
                        pltpu.VMEM((PEER_ECHUNK, PEER_TILE), F32),
                        pltpu.VMEM((PEER_ECHUNK, PEER_TILE), BF16)],
        compiler_params=_params("arbitrary", "arbitrary"),
        name="peer",
    )(xn, x1, u, vt, rk, e1, cnt, e0)


def _layer(x, norm1_w, w_in, w_alpha, b_alpha, out_norm_w, qn_w, kn_w, mix_scale, w_out,
           norm2_w, w_query, subkeys, u_tab, v_tab):
    B, S, D = x.shape
    T = B * S
    x2 = x.reshape(T, D)

    o_gr = 2 * GLA_QK + 2 * GLA_V
    w_r = jnp.concatenate([w_in[:, :o_gr], w_in[:, o_gr + GLA_RANK:], w_in[:, o_gr:o_gr + GLA_RANK],
                           jnp.zeros((D, 128 - GLA_RANK), w_in.dtype)], axis=1).astype(BF16)
    wa = jnp.concatenate([w_alpha, jnp.zeros((128 - GLA_RANK, GLA_QK), w_alpha.dtype)], axis=0)

    proj = _inproj(x2, norm1_w.reshape(1, D), w_r)
    o_gla = _gla(proj, B, S, wa, b_alpha.reshape(1, GLA_QK), out_norm_w.reshape(1, GLA_DV),
                 mix_scale[:GLA_V].reshape(1, GLA_V))

    half = MOBA_HD // 2
    inv = ROPE_THETA ** (-jnp.arange(half, dtype=F32) / half)
    ang = inv[:, None] * jnp.arange(S, dtype=F32)[None, :]
    qw = jnp.broadcast_to(qn_w.reshape(MOBA_HD, 1), (MOBA_HD, MOBA_BLOCK))
    kw = jnp.broadcast_to(kn_w.reshape(MOBA_HD, 1), (MOBA_HD, MOBA_BLOCK))
    qt, kh, vt, bias = _moba_prep(proj, B, S, qw, kw, jnp.cos(ang), jnp.sin(ang))
    o_moba_t = _moba(qt, kh, vt, bias, B, S)

    H, NK, hd = PEER_HEADS, PEER_NKEYS, PEER_QDIM // 2
    wq = w_query.reshape(D, H, 2, hd).transpose(0, 2, 1, 3).reshape(D, 2 * H * hd).astype(BF16)
    eye = jnp.eye(H, dtype=subkeys.dtype)
    skbig = jnp.einsum('hpkd,hg->pkhgd', subkeys, eye).reshape(2, NK * H, H * hd).astype(BF16)

    x1, xn, q = _outproj(x2, o_gla, o_moba_t, mix_scale[GLA_V:].reshape(1, MOBA_W),
                         w_out.astype(BF16), norm2_w.reshape(1, D), wq, B, S)
    rk, e1, cnt, e0 = _route(q, skbig)
    vt = v_tab.reshape(PEER_NCHUNK, PEER_ECHUNK, D).transpose(0, 2, 1).astype(BF16)
    out = _peer(xn, x1, u_tab.astype(BF16), vt, rk, e1, cnt, e0)
    return out.reshape(B, S, D)


def kernel(x, norm1_w, w_in, gla_w_alpha, gla_b_alpha, gla_out_norm_w, moba_q_norm_w, moba_k_norm_w,
           mix_scale, w_out, norm2_w, peer_w_query, peer_subkeys, peer_u, peer_v):
    assert norm1_w.shape[0] == 1, "single-layer kernel"
    return _layer(x, norm1_w[0], w_in[0], gla_w_alpha[0], gla_b_alpha[0], gla_out_norm_w[0],
                  moba_q_norm_w[0], moba_k_norm_w[0], mix_scale[0], w_out[0], norm2_w[0],
                  peer_w_query[0], peer_subkeys[0], peer_u[0], peer_v[0])
```

```python
import math

import jax
import jax.numpy as jnp
from jax import lax
from jax.experimental import pallas as pl
from jax.experimental.pallas import tpu as pltpu

D_MODEL = 1024
GLA_HEADS, GLA_DK, GLA_DV, GLA_RANK, GLA_TAU, GLA_CHUNK = 4, 64, 128, 16, 16.0, 64
MOBA_HEADS, MOBA_HD, MOBA_BLOCK, MOBA_TOPK = 8, 64, 256, 3
ROPE_THETA = 10000.0
PEER_HEADS, PEER_NKEYS, PEER_QDIM, PEER_TOPK = 8, 128, 256, 16
PEER_N = PEER_NKEYS * PEER_NKEYS
EPS = 1e-6

GLA_QK = GLA_HEADS * GLA_DK
GLA_V = GLA_HEADS * GLA_DV
MOBA_W = MOBA_HEADS * MOBA_HD
PROJ_W = 2 * GLA_QK + 2 * GLA_V + 3 * MOBA_W + 128
GR_COLBLOCK = (PROJ_W - 128) // 128

F32 = jnp.float32
BF16 = jnp.bfloat16
HIGHEST = lax.Precision.HIGHEST
NEG = -1e30
LOG2E = math.log2(math.e)

VMEM_LIMIT_BYTES = 56 * 1024 * 1024

IN_TILE = 512
GLA_TILE = 256
OUT_TILE = 256
ROUTE_TILE = 256
ROUTE_LANES = 128
PEER_TILE = 512
PEER_ECHUNK = 2048
PEER_ACT_SPLITS = (1024,)
PEER_STRIP_ROWS = 32
LANES = 256


def _params(*sem):
    return pltpu.CompilerParams(dimension_semantics=sem, vmem_limit_bytes=VMEM_LIMIT_BYTES)


def _nt_dot(a, b, precision=None):
    return lax.dot_general(a, b, (((1,), (1,)), ((), ())), precision=precision,
                           preferred_element_type=F32)


def _inproj_kernel(x_ref, nw_ref, w_ref, o_ref):
    x = x_ref[...]
    ms = jnp.mean(x * x, axis=-1, keepdims=True)
    xn = (x * lax.rsqrt(ms + EPS) * nw_ref[...]).astype(BF16)
    o_ref[...] = jnp.dot(xn, w_ref[...], preferred_element_type=F32)


def _inproj(x2, norm_w, w):
    T = x2.shape[0]
    return pl.pallas_call(
        _inproj_kernel,
        grid=(T // IN_TILE,),
        in_specs=[pl.BlockSpec((IN_TILE, D_MODEL), lambda i: (i, 0)),
                  pl.BlockSpec((1, D_MODEL), lambda i: (0, 0)),
                  pl.BlockSpec((D_MODEL, PROJ_W), lambda i: (0, 0))],
        out_specs=pl.BlockSpec((IN_TILE, PROJ_W), lambda i: (i, 0)),
        out_shape=jax.ShapeDtypeStruct((T, PROJ_W), F32),
        compiler_params=_params("arbitrary"),
        name="inproj",
    )(x2, norm_w, w)


def _gla_kernel(qk_ref, v_ref, gate_ref, gr_ref, wa_ref, ba_ref, nw_ref, ms_ref, o_ref, st_ref):
    @pl.when(pl.program_id(1) == 0)
    def _():
        st_ref[...] = jnp.zeros_like(st_ref)

    C = GLA_CHUNK
    z = jnp.dot(gr_ref[...], wa_ref[...], precision=HIGHEST, preferred_element_type=F32) + ba_ref[...]
    log_a = (jnp.minimum(z, 0.0) - jnp.log1p(jnp.exp(-jnp.abs(z)))) * (1.0 / GLA_TAU)
    row = lax.broadcasted_iota(jnp.int32, (C, C), 0)
    col = lax.broadcasted_iota(jnp.int32, (C, C), 1)
    causal = col <= row
    tril = causal.astype(F32)
    chunks = range(GLA_TILE // C)
    heads = range(GLA_HEADS)
    ks = [slice(h * GLA_DK, (h + 1) * GLA_DK) for h in heads]
    vs = [slice(h * GLA_DV, (h + 1) * GLA_DV) for h in heads]
    sl = [slice(c * C, (c + 1) * C) for c in chunks]
    q_d, k_in, k_st, decay = [], [], [], []
    for c in chunks:
        b = jnp.dot(tril, log_a[sl[c]], precision=HIGHEST, preferred_element_type=F32)
        b_last = b[C - 1:C, :]
        q = qk_ref[sl[c], 0:GLA_QK]
        k = qk_ref[sl[c], GLA_QK:2 * GLA_QK]
        q_d.append(q * jnp.exp(b) * (GLA_DK ** -0.5))
        k_in.append(k * jnp.exp(-b))
        k_st.append((k * jnp.exp(b_last - b)).astype(BF16))
        decay.append(jnp.exp(b_last))
    v = [[v_ref[sl[c], vs[h]] for h in heads] for c in chunks]
    a = [[jnp.where(causal, _nt_dot(q_d[c][:, ks[h]], k_in[c][:, ks[h]], precision=HIGHEST), 0.0)
          for h in heads] for c in chunks]
    o = [[jnp.dot(a[c][h].astype(BF16), v[c][h].astype(BF16), preferred_element_type=F32)
          for h in heads] for c in chunks]
    upd = [[jnp.dot(v[c][h].T.astype(BF16), k_st[c][:, ks[h]], preferred_element_type=F32)
            for h in heads] for c in chunks]
    st = [st_ref[h] for h in heads]
    for c in chunks:
        for h in heads:
            o[c][h] = o[c][h] + _nt_dot(q_d[c][:, ks[h]].astype(BF16), st[h].astype(BF16))
            st[h] = st[h] * decay[c][:, ks[h]] + upd[c][h]
    out = []
    for c in chunks:
        for h in heads:
            y = o[c][h]
            y = y * lax.rsqrt(jnp.mean(y * y, axis=-1, keepdims=True) + EPS) * nw_ref[...]
            g = gate_ref[sl[c], vs[h]]
            out.append((y * (g * jax.nn.sigmoid(g)) * ms_ref[:, vs[h]]).astype(BF16))
    for c in chunks:
        for h in heads:
            o_ref[sl[c], vs[h]] = out[c * GLA_HEADS + h]
    for h in heads:
        st_ref[h] = st[h]


def _gla(proj, B, S, w_alpha, b_alpha, out_norm_w, mix_scale_gla):
    nt = S // GLA_TILE
    tok = lambda b, j: b * nt + j
    return pl.pallas_call(
        _gla_kernel,
        grid=(B, nt),
        in_specs=[pl.BlockSpec((GLA_TILE, 2 * GLA_QK), lambda b, j: (tok(b, j), 0)),
                  pl.BlockSpec((GLA_TILE, GLA_V), lambda b, j: (tok(b, j), 1)),
                  pl.BlockSpec((GLA_TILE, GLA_V), lambda b, j: (tok(b, j), 2)),
                  pl.BlockSpec((GLA_TILE, 128), lambda b, j: (tok(b, j), GR_COLBLOCK)),
                  pl.BlockSpec((128, GLA_QK), lambda b, j: (0, 0)),
                  pl.BlockSpec((1, GLA_QK), lambda b, j: (0, 0)),
                  pl.BlockSpec((1, GLA_DV), lambda b, j: (0, 0)),
                  pl.BlockSpec((1, GLA_V), lambda b, j: (0, 0))],
        out_specs=pl.BlockSpec((GLA_TILE, GLA_V), lambda b, j: (tok(b, j), 0)),
        out_shape=jax.ShapeDtypeStruct((B * S, GLA_V), BF16),
        scratch_shapes=[pltpu.VMEM((GLA_HEADS, GLA_DV, GLA_DK), F32)],
        compiler_params=_params("arbitrary", "arbitrary"),
        name="gla",
    )(proj, proj, proj, proj, w_alpha, b_alpha, out_norm_w, mix_scale_gla)


def _norm_rope_t(xt, w_ref, cos_ref, sin_ref):
    half = MOBA_HD // 2
    outs = []
    for h in range(MOBA_HEADS):
        xh = xt[h * MOBA_HD:(h + 1) * MOBA_HD]
        ms = jnp.mean(xh * xh, axis=0, keepdims=True)
        y = xh * lax.rsqrt(ms + EPS) * w_ref[...]
        y1, y2 = y[:half], y[half:]
        c, s = cos_ref[...], sin_ref[...]
        outs.append(y1 * c - y2 * s)
        outs.append(y2 * c + y1 * s)
    return jnp.concatenate(outs, axis=0)


def _moba_prep_kernel(q_ref, k_ref, v_ref, qw_ref, kw_ref, cos_ref, sin_ref,
                      qt_ref, kh_ref, vt_ref, bias_ref, mt_ref):
    j = pl.program_id(1)
    nb = 16

    @pl.when(j == 0)
    def _():
        mt_ref[...] = jnp.zeros_like(mt_ref)

    qt = _norm_rope_t(q_ref[...].T, qw_ref, cos_ref, sin_ref)
    kt = _norm_rope_t(k_ref[...].T, kw_ref, cos_ref, sin_ref)
    qt_ref[0] = (qt * (MOBA_HD ** -0.5 * LOG2E)).astype(BF16)
    kn = kt.T
    for h in range(MOBA_HEADS):
        kh_ref[0, h] = kn[:, h * MOBA_HD:(h + 1) * MOBA_HD].astype(BF16)
    vt_ref[0, 0] = v_ref[...].T.astype(BF16)

    bs = jnp.dot(mt_ref[...], qt, precision=HIGHEST, preferred_element_type=F32)
    n_idx = lax.broadcasted_iota(jnp.int32, (nb, MOBA_BLOCK), 0)
    for h in range(MOBA_HEADS):
        sc = jnp.where(n_idx < j, bs[h * nb:(h + 1) * nb], -jnp.inf)
        sel = jnp.zeros((nb, MOBA_BLOCK), jnp.bool_)
        for r in range(MOBA_TOPK):
            m = jnp.max(sc, axis=0, keepdims=True)
            first = jnp.min(jnp.where(sc == m, n_idx, nb), axis=0, keepdims=True)
            hit = n_idx == first
            sel = jnp.logical_or(sel, jnp.logical_and(hit, r < j))
            sc = jnp.where(hit, -jnp.inf, sc)
        bias_ref[0, 0, h] = jnp.where(sel, 0.0, NEG)

    kbar = jnp.mean(kn, axis=0, keepdims=True)
    r_idx = lax.broadcasted_iota(jnp.int32, (MOBA_HEADS * nb, MOBA_W), 0)
    c_idx = lax.broadcasted_iota(jnp.int32, (MOBA_HEADS * nb, MOBA_W), 1)
    mine = jnp.logical_and(r_idx % nb == j, r_idx // nb == c_idx // MOBA_HD)
    mt_ref[...] = jnp.where(mine, kbar, mt_ref[...])


def _moba_prep(proj, B, S, qw, kw, cos_t, sin_t):
    nb = S // MOBA_BLOCK
    assert nb <= 16
    tok = lambda b, j: b * nb + j
    half = MOBA_HD // 2
    return pl.pallas_call(
        _moba_prep_kernel,
        grid=(B, nb),
        in_specs=[pl.BlockSpec((MOBA_BLOCK, MOBA_W), lambda b, j: (tok(b, j), 3)),
                  pl.BlockSpec((MOBA_BLOCK, MOBA_W), lambda b, j: (tok(b, j), 4)),
                  pl.BlockSpec((MOBA_BLOCK, MOBA_W), lambda b, j: (tok(b, j), 5)),
                  pl.BlockSpec((MOBA_HD, MOBA_BLOCK), lambda b, j: (0, 0)),
                  pl.BlockSpec((MOBA_HD, MOBA_BLOCK), lambda b, j: (0, 0)),
                  pl.BlockSpec((half, MOBA_BLOCK), lambda b, j: (0, j)),
                  pl.BlockSpec((half, MOBA_BLOCK), lambda b, j: (0, j))],
        out_specs=[pl.BlockSpec((1, MOBA_W, MOBA_BLOCK), lambda b, j: (b, 0, j)),
                   pl.BlockSpec((1, MOBA_HEADS, MOBA_BLOCK, MOBA_HD), lambda b, j: (b, 0, j, 0)),
                   pl.BlockSpec((1, 1, MOBA_W, MOBA_BLOCK), lambda b, j: (b, j, 0, 0)),
                   pl.BlockSpec((1, 1, MOBA_HEADS, 16, MOBA_BLOCK), lambda b, j: (b, j, 0, 0, 0))],
        out_shape=[jax.ShapeDtypeStruct((B, MOBA_W, S), BF16),
                   jax.ShapeDtypeStruct((B, MOBA_HEADS, S, MOBA_HD), BF16),
                   jax.ShapeDtypeStruct((B, nb, MOBA_W, MOBA_BLOCK), BF16),
                   jax.ShapeDtypeStruct((B, nb, MOBA_HEADS, 16, MOBA_BLOCK), F32)],
        scratch_shapes=[pltpu.VMEM((MOBA_HEADS * 16, MOBA_W), F32)],
        compiler_params=_params("arbitrary", "arbitrary"),
        name="moba_prep",
    )(proj, proj, proj, qw, kw, cos_t, sin_t)


def _moba_kernel(qt_ref, k_ref, vt_ref, bias_ref, o_ref, acc_ref, m_ref, l_ref, s0_ref, s1_ref):
    j = pl.program_id(1)
    hd = MOBA_HD
    kpos = lax.broadcasted_iota(jnp.int32, (MOBA_BLOCK, MOBA_BLOCK), 0)
    qpos = lax.broadcasted_iota(jnp.int32, (MOBA_BLOCK, MOBA_BLOCK), 1)
    causal = kpos <= qpos
    own = pl.ds(pl.multiple_of(j * MOBA_BLOCK, MOBA_BLOCK), MOBA_BLOCK)
    heads = range(MOBA_HEADS)
    rows = [slice(h * hd, (h + 1) * hd) for h in heads]
    s = [jnp.where(causal, jnp.dot(k_ref[0, h, own, :], qt_ref[0, rows[h], :], preferred_element_type=F32),
                   -jnp.inf) for h in heads]
    m = [jnp.max(s[h], axis=0, keepdims=True) for h in heads]
    p = [jnp.exp2(s[h] - m[h]) for h in heads]
    l = [jnp.sum(p[h], axis=0, keepdims=True) for h in heads]
    pv = [jnp.dot(vt_ref[0, j, rows[h], :], p[h].astype(BF16), preferred_element_type=F32)
          for h in heads]
    for h in heads:
        m_ref[h] = m[h]
        l_ref[h] = l[h]
        acc_ref[h] = pv[h]

    nb = k_ref.shape[2] // MOBA_BLOCK

    def scores(n, s_ref):
        nc = jnp.minimum(n, nb - 1)
        blk = pl.ds(pl.multiple_of(nc * MOBA_BLOCK, MOBA_BLOCK), MOBA_BLOCK)
        for h in heads:
            s_ref[h] = (jnp.dot(k_ref[0, h, blk, :], qt_ref[0, rows[h], :], preferred_element_type=F32)
                        + bias_ref[0, 0, h, pl.ds(nc, 1), :])

    def consume(n, s_ref):
        s = {h: s_ref[h] for h in heads}
        m_old = {h: m_ref[h] for h in heads}
        m_new = {h: jnp.maximum(m_old[h], jnp.max(s[h], axis=0, keepdims=True)) for h in heads}
        alpha = {h: jnp.exp2(m_old[h] - m_new[h]) for h in heads}
        p = {h: jnp.exp2(s[h] - m_new[h]) for h in heads}
        l_new = {h: alpha[h] * l_ref[h] + jnp.sum(p[h], axis=0, keepdims=True) for h in heads}
        pv = {h: jnp.dot(vt_ref[0, n, rows[h], :], p[h].astype(BF16), preferred_element_type=F32)
              for h in heads}
        acc_new = {h: acc_ref[h] * alpha[h] + pv[h] for h in heads}
        for h in heads:
            m_ref[h] = m_new[h]
            l_ref[h] = l_new[h]
            acc_ref[h] = acc_new[h]

    scores(0, s0_ref)

    def pair(nn, carry):
        n = 2 * nn
        scores(n + 1, s1_ref)
        consume(n, s0_ref)
        scores(n + 2, s0_ref)
        consume(n + 1, s1_ref)
        return carry

    lax.fori_loop(0, j // 2, pair, 0)

    @pl.when(j % 2 == 1)
    def _():
        consume(j - 1, s0_ref)

    for h in range(MOBA_HEADS):
        o_ref[0, h * hd:(h + 1) * hd, :] = (acc_ref[h] / l_ref[h]).astype(BF16)


def _moba(qt, kh, vt, bias, B, S):
    nb = S // MOBA_BLOCK
    return pl.pallas_call(
        _moba_kernel,
        grid=(B, nb),
        in_specs=[pl.BlockSpec((1, MOBA_W, MOBA_BLOCK), lambda b, j: (b, 0, j)),
                  pl.BlockSpec((1, MOBA_HEADS, S, MOBA_HD), lambda b, j: (b, 0, 0, 0)),
                  pl.BlockSpec((1, nb, MOBA_W, MOBA_BLOCK), lambda b, j: (b, 0, 0, 0)),
                  pl.BlockSpec((1, 1, MOBA_HEADS, 16, MOBA_BLOCK), lambda b, j: (b, j, 0, 0, 0))],
        out_specs=pl.BlockSpec((1, MOBA_W, MOBA_BLOCK), lambda b, j: (b, 0, j)),
        out_shape=jax.ShapeDtypeStruct((B, MOBA_W, S), BF16),
        scratch_shapes=[pltpu.VMEM((MOBA_HEADS, MOBA_HD, MOBA_BLOCK), F32),
                        pltpu.VMEM((MOBA_HEADS, 1, MOBA_BLOCK), F32),
                        pltpu.VMEM((MOBA_HEADS, 1, MOBA_BLOCK), F32),
                        pltpu.VMEM((MOBA_HEADS, MOBA_BLOCK, MOBA_BLOCK), F32),
                        pltpu.VMEM((MOBA_HEADS, MOBA_BLOCK, MOBA_BLOCK), F32)],
        compiler_params=_params("arbitrary", "arbitrary"),
        name="moba",
    )(qt, kh, vt, bias)


def _outproj_kernel(x_ref, og_ref, omt_ref, msm_ref, wo_ref, n2_ref, wq_ref, x1_ref, xn_ref, q_ref):
    om = (omt_ref[0].astype(F32).T * msm_ref[...]).astype(BF16)
    x1 = (x_ref[...]
          + jnp.dot(og_ref[...], wo_ref[0:GLA_V, :], preferred_element_type=F32)
          + jnp.dot(om, wo_ref[GLA_V:, :], preferred_element_type=F32))
    x1_ref[...] = x1
    ms = jnp.mean(x1 * x1, axis=-1, keepdims=True)
    xn = (x1 * lax.rsqrt(ms + EPS) * n2_ref[...]).astype(BF16)
    xn_ref[...] = xn
    q_ref[...] = jnp.dot(xn, wq_ref[...], preferred_element_type=F32).astype(BF16)


def _outproj(x2, o_gla, o_moba_t, ms_moba, w_out, norm2_w, wq, B, S):
    T = B * S
    nt = S // OUT_TILE
    QW = PEER_HEADS * PEER_QDIM
    return pl.pallas_call(
        _outproj_kernel,
        grid=(T // OUT_TILE,),
        in_specs=[pl.BlockSpec((OUT_TILE, D_MODEL), lambda i: (i, 0)),
                  pl.BlockSpec((OUT_TILE, GLA_V), lambda i: (i, 0)),
                  pl.BlockSpec((1, MOBA_W, OUT_TILE), lambda i: (i // nt, 0, i % nt)),
                  pl.BlockSpec((1, MOBA_W), lambda i: (0, 0)),
                  pl.BlockSpec((GLA_V + MOBA_W, D_MODEL), lambda i: (0, 0)),
                  pl.BlockSpec((1, D_MODEL), lambda i: (0, 0)),
                  pl.BlockSpec((D_MODEL, QW), lambda i: (0, 0))],
        out_specs=[pl.BlockSpec((OUT_TILE, D_MODEL), lambda i: (i, 0)),
                   pl.BlockSpec((OUT_TILE, D_MODEL), lambda i: (i, 0)),
                   pl.BlockSpec((OUT_TILE, QW), lambda i: (i, 0))],
        out_shape=[jax.ShapeDtypeStruct((T, D_MODEL), F32),
                   jax.ShapeDtypeStruct((T, D_MODEL), BF16),
                   jax.ShapeDtypeStruct((T, QW), BF16)],
        compiler_params=_params("arbitrary"),
        name="outproj",
    )(x2, o_gla, o_moba_t, ms_moba, w_out, norm2_w, wq)


_PAIRS = [(r, q) for r in range(PEER_TOPK + 1) for q in range(PEER_TOPK + 1)
          if (r + 1) * (q + 1) <= PEER_TOPK + 1]


def _tree(op, xs):
    xs = list(xs)
    while len(xs) > 1:
        xs = [op(xs[i], xs[i + 1]) if i + 1 < len(xs) else xs[i] for i in range(0, len(xs), 2)]
    return xs[0]


def _route_kernel(q_ref, sk_ref, rk_ref, e1_ref, cnt_ref, e0_ref, sc_ref, tmp_ref):
    H, NK = PEER_HEADS, PEER_NKEYS
    hq = H * PEER_QDIM // 2
    G = 8
    NTOP = PEER_TOPK + 1
    for p in range(2):
        sc = _nt_dot(sk_ref[p], q_ref[:, p * hq:(p + 1) * hq])
        for lt in range(ROUTE_TILE // ROUTE_LANES):
            sc_ref[p, lt] = sc[:, lt * ROUTE_LANES:(lt + 1) * ROUTE_LANES]
    neg = jnp.full((H, ROUTE_LANES), -jnp.inf, F32)

    def insert(top, v):
        out = []
        for t in top:
            out.append(jnp.maximum(t, v))
            v = jnp.minimum(t, v)
        return out

    for lt in range(ROUTE_TILE // ROUTE_LANES):
        lanes = slice(lt * ROUTE_LANES, (lt + 1) * ROUTE_LANES)

        def largest(p, lt=lt):
            def body(g, top):
                v = sc_ref[p, lt, pl.ds(pl.multiple_of(g * (G * H), G * H), G * H), :]
                top = list(top)
                for k in range(G):
                    top = insert(top, v[k * H:(k + 1) * H])
                return tuple(top)
            return lax.fori_loop(0, NK // G, body, (neg,) * NTOP)

        tops = (largest(0), largest(1))
        best = [neg] * NTOP
        for r, q in _PAIRS:
            best = insert(best, tops[0][r] + tops[1][q])
        tau = 0.5 * (best[PEER_TOPK - 1] + best[PEER_TOPK])
        top = tops[0][0] + tops[1][0]
        zinv = 0.5 / _tree(jnp.add, [jnp.where(tops[0][r] + tops[1][q] > tau,
                                               jnp.exp(tops[0][r] + tops[1][q] - top), 0.0)
                                     for r, q in _PAIRS])
        thr = [tau - b for b in tops[1][:PEER_TOPK]]
        top1 = tops[1][:PEER_TOPK]

        def half0(g, carry, lt=lt, tops=tops, thr=thr, zinv=zinv):
            base = pl.multiple_of(g * (G * H), G * H)
            v0 = sc_ref[0, lt, pl.ds(base, G * H), :]
            for k in range(G):
                a = v0[k * H:(k + 1) * H]
                rows = pl.ds(pl.multiple_of(base + k * H, H), H)
                tmp_ref[0, rows, :] = _tree(jnp.add, [jnp.where(a > t, 1.0, 0.0) for t in thr])
                tmp_ref[1, rows, :] = jnp.exp(a - tops[0][0]) * zinv
            return carry

        def half1(g, carry, lt=lt, tops=tops, top1=top1):
            base = pl.multiple_of(g * (G * H), G * H)
            v1 = sc_ref[1, lt, pl.ds(base, G * H), :]
            for k in range(G):
                b = v1[k * H:(k + 1) * H]
                rows = pl.ds(pl.multiple_of(base + k * H, H), H)
                tmp_ref[2, rows, :] = 1.0 + _tree(jnp.add, [jnp.where(b < t, 1.0, 0.0) for t in top1])
                tmp_ref[3, rows, :] = jnp.exp(b - tops[1][0])
            return carry

        lax.fori_loop(0, NK // G, half0, 0)
        lax.fori_loop(0, NK // G, half1, 0)
        for h in range(H):
            cnt_ref[h, :, lanes] = tmp_ref[0, pl.ds(h, NK, stride=H), :]
            e0_ref[h, :, lanes] = tmp_ref[1, pl.ds(h, NK, stride=H), :]
            rk_ref[h, :, lanes] = tmp_ref[2, pl.ds(h, NK, stride=H), :].astype(BF16)
            e1_ref[h, :, lanes] = tmp_ref[3, pl.ds(h, NK, stride=H), :].astype(BF16)


def _route(q, skbig):
    T = q.shape[0]
    H, NK = PEER_HEADS, PEER_NKEYS
    hq = H * PEER_QDIM // 2
    spec = pl.BlockSpec((H, NK, ROUTE_TILE), lambda i: (0, 0, i))
    return pl.pallas_call(
        _route_kernel,
        grid=(T // ROUTE_TILE,),
        in_specs=[pl.BlockSpec((ROUTE_TILE, 2 * hq), lambda i: (i, 0)),
                  pl.BlockSpec((2, NK * H, hq), lambda i: (0, 0, 0))],
        out_specs=[spec, spec, spec, spec],
        out_shape=[jax.ShapeDtypeStruct((H, NK, T), BF16), jax.ShapeDtypeStruct((H, NK, T), BF16),
                   jax.ShapeDtypeStruct((H, NK, T), F32), jax.ShapeDtypeStruct((H, NK, T), F32)],
        scratch_shapes=[pltpu.VMEM((2, ROUTE_TILE // ROUTE_LANES, NK * H, ROUTE_LANES), F32),
                        pltpu.VMEM((4, NK * H, ROUTE_LANES), F32)],
        compiler_params=_params("arbitrary"),
        name="peer_route",
    )(q, skbig)


PEER_NCHUNK = PEER_N // PEER_ECHUNK


def _peer_kernel(xn_ref, x1_ref, u_ref, vt_ref, rk_ref, e1_ref, cnt_ref, e0_ref, o_ref,
                 acc_ref, act_ref, w_ref):
    c = pl.program_id(1)
    NK = PEER_NKEYS
    nsub = PEER_ECHUNK // NK
    RB = PEER_STRIP_ROWS

    @pl.when(c == 0)
    def _():
        acc_ref[...] = jnp.zeros_like(acc_ref)

    bounds = (0,) + PEER_ACT_SPLITS + (PEER_ECHUNK,)
    for ii in range(nsub):
        if ii * NK in bounds[:-1]:
            hs = slice(ii * NK, bounds[bounds.index(ii * NK) + 1])
            act_ref[hs, :] = _nt_dot(u_ref[hs, :], xn_ref[...])
        i = c * nsub + ii
        cnt_rows = [cnt_ref[h, pl.ds(i, 1), :] for h in range(PEER_HEADS)]
        e0_rows = [e0_ref[h, pl.ds(i, 1), :] for h in range(PEER_HEADS)]
        for lc in range(PEER_TILE // LANES):
            lanes = slice(lc * LANES, (lc + 1) * LANES)
            cnts = [jnp.broadcast_to(r[:, lanes], (RB, LANES)).astype(BF16) for r in cnt_rows]
            e0s = [jnp.broadcast_to(r[:, lanes], (RB, LANES)).astype(BF16) for r in e0_rows]
            for rb in range(NK // RB):
                rows = slice(rb * RB, (rb + 1) * RB)
                wsum = _tree(jnp.add, [
                    e0s[h] * jnp.where(rk_ref[h, rows, lanes] <= cnts[h], e1_ref[h, rows, lanes], 0.0)
                    for h in range(PEER_HEADS)])
                er = slice(ii * NK + rb * RB, ii * NK + (rb + 1) * RB)
                a = act_ref[er, lanes]
                g2 = a * (1.0 + lax.erf(a * (1.0 / math.sqrt(2.0))))
                w_ref[er, lanes] = g2.astype(BF16) * wsum
    acc_ref[...] += jnp.dot(vt_ref[0], w_ref[...], preferred_element_type=F32)

    @pl.when(c == PEER_NCHUNK - 1)
    def _():
        o_ref[...] = x1_ref[...] + acc_ref[...].T


def _peer(xn, x1, u, vt, rk, e1, cnt, e0):
    T = xn.shape[0]
    H, NK = PEER_HEADS, PEER_NKEYS
    rspec = pl.BlockSpec((H, NK, PEER_TILE), lambda t, c: (0, 0, t))
    return pl.pallas_call(
        _peer_kernel,
        grid=(T // PEER_TILE, PEER_NCHUNK),
        in_specs=[pl.BlockSpec((PEER_TILE, D_MODEL), lambda t, c: (t, 0)),
                  pl.BlockSpec((PEER_TILE, D_MODEL), lambda t, c: (t, 0)),
                  pl.BlockSpec((PEER_ECHUNK, D_MODEL), lambda t, c: (c, 0)),
                  pl.BlockSpec((1, D_MODEL, PEER_ECHUNK), lambda t, c: (c, 0, 0)),
                  rspec, rspec, rspec, rspec],
        out_specs=pl.BlockSpec((PEER_TILE, D_MODEL), lambda t, c: (t, 0)),
        out_shape=jax.ShapeDtypeStruct((T, D_MODEL), F32),
        scratch_shapes=[pltpu.VMEM((D_MODEL, PEER_TILE), F32),
                        pltpu.VMEM((PEER_ECHUNK, PEER_TILE), F32),
                        pltpu.VMEM((PEER_ECHUNK, PEER_TILE), BF16)],
        compiler_params=_params("arbitrary", "arbitrary"),
        name="peer",
    )(xn, x1, u, vt, rk, e1, cnt, e0)


def _layer(x, norm1_w, w_in, w_alpha, b_alpha, out_norm_w, qn_w, kn_w, mix_scale, w_out,
           norm2_w, w_query, subkeys, u_tab, v_tab):
    B, S, D = x.shape
    T = B * S
    x2 = x.reshape(T, D)

    o_gr = 2 * GLA_QK + 2 * GLA_V
    w_r = jnp.concatenate([w_in[:, :o_gr], w_in[:, o_gr + GLA_RANK:], w_in[:, o_gr:o_gr + GLA_RANK],
                           jnp.zeros((D, 128 - GLA_RANK), w_in.dtype)], axis=1).astype(BF16)
    wa = jnp.concatenate([w_alpha, jnp.zeros((128 - GLA_RANK, GLA_QK), w_alpha.dtype)], axis=0)

    proj = _inproj(x2, norm1_w.reshape(1, D), w_r)
    o_gla = _gla(proj, B, S, wa, b_alpha.reshape(1, GLA_QK), out_norm_w.reshape(1, GLA_DV),
                 mix_scale[:GLA_V].reshape(1, GLA_V))

    half = MOBA_HD // 2
    inv = ROPE_THETA ** (-jnp.arange(half, dtype=F32) / half)
    ang = inv[:, None] * jnp.arange(S, dtype=F32)[None, :]
    qw = jnp.broadcast_to(qn_w.reshape(MOBA_HD, 1), (MOBA_HD, MOBA_BLOCK))
    kw = jnp.broadcast_to(kn_w.reshape(MOBA_HD, 1), (MOBA_HD, MOBA_BLOCK))
    qt, kh, vt, bias = _moba_prep(proj, B, S, qw, kw, jnp.cos(ang), jnp.sin(ang))
    o_moba_t = _moba(qt, kh, vt, bias, B, S)

    H, NK, hd = PEER_HEADS, PEER_NKEYS, PEER_QDIM // 2
    wq = w_query.reshape(D, H, 2, hd).transpose(0, 2, 1, 3).reshape(D, 2 * H * hd).astype(BF16)
    eye = jnp.eye(H, dtype=subkeys.dtype)
    skbig = jnp.einsum('hpkd,hg->pkhgd', subkeys, eye).reshape(2, NK * H, H * hd).astype(BF16)

    x1, xn, q = _outproj(x2, o_gla, o_moba_t, mix_scale[GLA_V:].reshape(1, MOBA_W),
                         w_out.astype(BF16), norm2_w.reshape(1, D), wq, B, S)
    rk, e1, cnt, e0 = _route(q, skbig)
    vt = v_tab.reshape(PEER_NCHUNK, PEER_ECHUNK, D).transpose(0, 2, 1).astype(BF16)
    out = _peer(xn, x1, u_tab.astype(BF16), vt, rk, e1, cnt, e0)
    return out.reshape(B, S, D)


def kernel(x, norm1_w, w_in, gla_w_alpha, gla_b_alpha, gla_out_norm_w, moba_q_norm_w, moba_k_norm_w,
           mix_scale, w_out, norm2_w, peer_w_query, peer_subkeys, peer_u, peer_v):
    assert norm1_w.shape[0] == 1, "single-layer kernel"
    return _layer(x, norm1_w[0], w_in[0], gla_w_alpha[0], gla_b_alpha[0], gla_out_norm_w[0],
                  moba_q_norm_w[0], moba_k_norm_w[0], mix_scale[0], w_out[0], norm2_w[0],
                  peer_w_query[0], peer_subkeys[0], peer_u[0], peer_v[0])
```

```python
import math

import jax
import jax.numpy as jnp
from jax import lax
from jax.experimental import pallas as pl
from jax.experimental.pallas import tpu as pltpu

D_MODEL = 1024
GLA_HEADS, GLA_DK, GLA_DV, GLA_RANK, GLA_TAU, GLA_CHUNK = 4, 64, 128, 16, 16.0, 64
MOBA_HEADS, MOBA_HD, MOBA_BLOCK, MOBA_TOPK = 8, 64, 256, 3
ROPE_THETA = 10000.0
PEER_HEADS, PEER_NKEYS, PEER_QDIM, PEER_TOPK = 8, 128, 256, 16
PEER_N = PEER_NKEYS * PEER_NKEYS
EPS = 1e-6

GLA_QK = GLA_HEADS * GLA_DK
GLA_V = GLA_HEADS * GLA_DV
MOBA_W = MOBA_HEADS * MOBA_HD
PROJ_W = 2 * GLA_QK + 2 * GLA_V + 3 * MOBA_W + 128
GR_COLBLOCK = (PROJ_W - 128) // 128

F32 = jnp.float32
BF16 = jnp.bfloat16
HIGHEST = lax.Precision.HIGHEST
NEG = -1e30
LOG2E = math.log2(math.e)

VMEM_LIMIT_BYTES = 56 * 1024 * 1024

IN_TILE = 512
GLA_TILE = 256
MOBA_VROWS = MOBA_HD + 16
OUT_TILE = 256
ROUTE_TILE = 256
ROUTE_LANES = 128
PEER_TILE = 512
PEER_ECHUNK = 2048
PEER_ACT_SPLITS = (1024,)
PEER_STRIP_ROWS = 32
LANES = 256


def _params(*sem):
    return pltpu.CompilerParams(dimension_semantics=sem, vmem_limit_bytes=VMEM_LIMIT_BYTES)


def _nt_dot(a, b, precision=None):
    return lax.dot_general(a, b, (((1,), (1,)), ((), ())), precision=precision,
                           preferred_element_type=F32)


def _inproj_kernel(x_ref, nw_ref, w_ref, o_ref):
    x = x_ref[...]
    ms = jnp.mean(x * x, axis=-1, keepdims=True)
    xn = (x * lax.rsqrt(ms + EPS) * nw_ref[...]).astype(BF16)
    o_ref[...] = jnp.dot(xn, w_ref[...], preferred_element_type=F32)


def _inproj(x2, norm_w, w):
    T = x2.shape[0]
    return pl.pallas_call(
        _inproj_kernel,
        grid=(T // IN_TILE,),
        in_specs=[pl.BlockSpec((IN_TILE, D_MODEL), lambda i: (i, 0)),
                  pl.BlockSpec((1, D_MODEL), lambda i: (0, 0)),
                  pl.BlockSpec((D_MODEL, PROJ_W), lambda i: (0, 0))],
        out_specs=pl.BlockSpec((IN_TILE, PROJ_W), lambda i: (i, 0)),
        out_shape=jax.ShapeDtypeStruct((T, PROJ_W), F32),
        compiler_params=_params("arbitrary"),
        name="inproj",
    )(x2, norm_w, w)


def _gla_kernel(qk_ref, v_ref, gate_ref, gr_ref, wa_ref, ba_ref, nw_ref, ms_ref, o_ref, st_ref):
    @pl.when(pl.program_id(1) == 0)
    def _():
        st_ref[...] = jnp.zeros_like(st_ref)

    C = GLA_CHUNK
    z = jnp.dot(gr_ref[...], wa_ref[...], precision=HIGHEST, preferred_element_type=F32) + ba_ref[...]
    log_a = (jnp.minimum(z, 0.0) - jnp.log1p(jnp.exp(-jnp.abs(z)))) * (1.0 / GLA_TAU)
    row = lax.broadcasted_iota(jnp.int32, (C, C), 0)
    col = lax.broadcasted_iota(jnp.int32, (C, C), 1)
    causal = col <= row
    tril = causal.astype(F32)
    chunks = range(GLA_TILE // C)
    heads = range(GLA_HEADS)
    ks = [slice(h * GLA_DK, (h + 1) * GLA_DK) for h in heads]
    vs = [slice(h * GLA_DV, (h + 1) * GLA_DV) for h in heads]
    sl = [slice(c * C, (c + 1) * C) for c in chunks]
    q_d, k_in, k_st, decay = [], [], [], []
    for c in chunks:
        b = jnp.dot(tril, log_a[sl[c]], precision=HIGHEST, preferred_element_type=F32)
        b_last = b[C - 1:C, :]
        q = qk_ref[sl[c], 0:GLA_QK]
        k = qk_ref[sl[c], GLA_QK:2 * GLA_QK]
        q_d.append(q * jnp.exp(b) * (GLA_DK ** -0.5))
        k_in.append(k * jnp.exp(-b))
        k_st.append((k * jnp.exp(b_last - b)).astype(BF16))
        decay.append(jnp.exp(b_last))
    v = [[v_ref[sl[c], vs[h]] for h in heads] for c in chunks]
    a = [[jnp.where(causal, _nt_dot(q_d[c][:, ks[h]], k_in[c][:, ks[h]], precision=HIGHEST), 0.0)
          for h in heads] for c in chunks]
    o = [[jnp.dot(a[c][h].astype(BF16), v[c][h].astype(BF16), preferred_element_type=F32)
          for h in heads] for c in chunks]
    upd = [[jnp.dot(v[c][h].T.astype(BF16), k_st[c][:, ks[h]], preferred_element_type=F32)
            for h in heads] for c in chunks]
    st = [st_ref[h] for h in heads]
    for c in chunks:
        for h in heads:
            o[c][h] = o[c][h] + _nt_dot(q_d[c][:, ks[h]].astype(BF16), st[h].astype(BF16))
            st[h] = st[h] * decay[c][:, ks[h]] + upd[c][h]
    out = []
    for c in chunks:
        for h in heads:
            y = o[c][h]
            y = y * lax.rsqrt(jnp.mean(y * y, axis=-1, keepdims=True) + EPS) * nw_ref[...]
            g = gate_ref[sl[c], vs[h]]
            out.append((y * (g * jax.nn.sigmoid(g)) * ms_ref[:, vs[h]]).astype(BF16))
    for c in chunks:
        for h in heads:
            o_ref[sl[c], vs[h]] = out[c * GLA_HEADS + h]
    for h in heads:
        st_ref[h] = st[h]


def _gla(proj, B, S, w_alpha, b_alpha, out_norm_w, mix_scale_gla):
    nt = S // GLA_TILE
    tok = lambda b, j: b * nt + j
    return pl.pallas_call(
        _gla_kernel,
        grid=(B, nt),
        in_specs=[pl.BlockSpec((GLA_TILE, 2 * GLA_QK), lambda b, j: (tok(b, j), 0)),
                  pl.BlockSpec((GLA_TILE, GLA_V), lambda b, j: (tok(b, j), 1)),
                  pl.BlockSpec((GLA_TILE, GLA_V), lambda b, j: (tok(b, j), 2)),
                  pl.BlockSpec((GLA_TILE, 128), lambda b, j: (tok(b, j), GR_COLBLOCK)),
                  pl.BlockSpec((128, GLA_QK), lambda b, j: (0, 0)),
                  pl.BlockSpec((1, GLA_QK), lambda b, j: (0, 0)),
                  pl.BlockSpec((1, GLA_DV), lambda b, j: (0, 0)),
                  pl.BlockSpec((1, GLA_V), lambda b, j: (0, 0))],
        out_specs=pl.BlockSpec((GLA_TILE, GLA_V), lambda b, j: (tok(b, j), 0)),
        out_shape=jax.ShapeDtypeStruct((B * S, GLA_V), BF16),
        scratch_shapes=[pltpu.VMEM((GLA_HEADS, GLA_DV, GLA_DK), F32)],
        compiler_params=_params("arbitrary", "arbitrary"),
        name="gla",
    )(proj, proj, proj, proj, w_alpha, b_alpha, out_norm_w, mix_scale_gla)


def _norm_rope_t(xt, w_ref, cos_ref, sin_ref):
    half = MOBA_HD // 2
    outs = []
    for h in range(MOBA_HEADS):
        xh = xt[h * MOBA_HD:(h + 1) * MOBA_HD]
        ms = jnp.mean(xh * xh, axis=0, keepdims=True)
        y = xh * lax.rsqrt(ms + EPS) * w_ref[...]
        y1, y2 = y[:half], y[half:]
        c, s = cos_ref[...], sin_ref[...]
        outs.append(y1 * c - y2 * s)
        outs.append(y2 * c + y1 * s)
    return jnp.concatenate(outs, axis=0)


def _moba_prep_kernel(q_ref, k_ref, v_ref, qw_ref, kw_ref, cos_ref, sin_ref,
                      qt_ref, kh_ref, vt_ref, bias_ref, mt_ref):
    j = pl.program_id(1)
    nb = 16

    @pl.when(j == 0)
    def _():
        mt_ref[...] = jnp.zeros_like(mt_ref)

    qt = _norm_rope_t(q_ref[...].T, qw_ref, cos_ref, sin_ref)
    kt = _norm_rope_t(k_ref[...].T, kw_ref, cos_ref, sin_ref)
    qt_ref[0] = (qt * (MOBA_HD ** -0.5 * LOG2E)).astype(BF16)
    kn = kt.T
    for h in range(MOBA_HEADS):
        kh_ref[0, h] = kn[:, h * MOBA_HD:(h + 1) * MOBA_HD].astype(BF16)
    vtt = v_ref[...].T
    ones_tile = (lax.broadcasted_iota(jnp.int32, (MOBA_VROWS - MOBA_HD, MOBA_BLOCK), 0) == 0).astype(BF16)
    for h in range(MOBA_HEADS):
        vt_ref[0, 0, h * MOBA_VROWS:h * MOBA_VROWS + MOBA_HD] = vtt[h * MOBA_HD:(h + 1) * MOBA_HD].astype(BF16)
        vt_ref[0, 0, h * MOBA_VROWS + MOBA_HD:(h + 1) * MOBA_VROWS] = ones_tile

    bs = jnp.dot(mt_ref[...], qt, precision=HIGHEST, preferred_element_type=F32)
    n_idx = lax.broadcasted_iota(jnp.int32, (nb, MOBA_BLOCK), 0)
    for h in range(MOBA_HEADS):
        sc = jnp.where(n_idx < j, bs[h * nb:(h + 1) * nb], -jnp.inf)
        sel = jnp.zeros((nb, MOBA_BLOCK), jnp.bool_)
        for r in range(MOBA_TOPK):
            m = jnp.max(sc, axis=0, keepdims=True)
            first = jnp.min(jnp.where(sc == m, n_idx, nb), axis=0, keepdims=True)
            hit = n_idx == first
            sel = jnp.logical_or(sel, jnp.logical_and(hit, r < j))
            sc = jnp.where(hit, -jnp.inf, sc)
        bias_ref[0, 0, h] = jnp.where(sel, 0.0, NEG)

    kbar = jnp.mean(kn, axis=0, keepdims=True)
    r_idx = lax.broadcasted_iota(jnp.int32, (MOBA_HEADS * nb, MOBA_W), 0)
    c_idx = lax.broadcasted_iota(jnp.int32, (MOBA_HEADS * nb, MOBA_W), 1)
    mine = jnp.logical_and(r_idx % nb == j, r_idx // nb == c_idx // MOBA_HD)
    mt_ref[...] = jnp.where(mine, kbar, mt_ref[...])


def _moba_prep(proj, B, S, qw, kw, cos_t, sin_t):
    nb = S // MOBA_BLOCK
    assert nb <= 16
    tok = lambda b, j: b * nb + j
    half = MOBA_HD // 2
    return pl.pallas_call(
        _moba_prep_kernel,
        grid=(B, nb),
        in_specs=[pl.BlockSpec((MOBA_BLOCK, MOBA_W), lambda b, j: (tok(b, j), 3)),
                  pl.BlockSpec((MOBA_BLOCK, MOBA_W), lambda b, j: (tok(b, j), 4)),
                  pl.BlockSpec((MOBA_BLOCK, MOBA_W), lambda b, j: (tok(b, j), 5)),
                  pl.BlockSpec((MOBA_HD, MOBA_BLOCK), lambda b, j: (0, 0)),
                  pl.BlockSpec((MOBA_HD, MOBA_BLOCK), lambda b, j: (0, 0)),
                  pl.BlockSpec((half, MOBA_BLOCK), lambda b, j: (0, j)),
                  pl.BlockSpec((half, MOBA_BLOCK), lambda b, j: (0, j))],
        out_specs=[pl.BlockSpec((1, MOBA_W, MOBA_BLOCK), lambda b, j: (b, 0, j)),
                   pl.BlockSpec((1, MOBA_HEADS, MOBA_BLOCK, MOBA_HD), lambda b, j: (b, 0, j, 0)),
                   pl.BlockSpec((1, 1, MOBA_HEADS * MOBA_VROWS, MOBA_BLOCK), lambda b, j: (b, j, 0, 0)),
                   pl.BlockSpec((1, 1, MOBA_HEADS, 16, MOBA_BLOCK), lambda b, j: (b, j, 0, 0, 0))],
        out_shape=[jax.ShapeDtypeStruct((B, MOBA_W, S), BF16),
                   jax.ShapeDtypeStruct((B, MOBA_HEADS, S, MOBA_HD), BF16),
                   jax.ShapeDtypeStruct((B, nb, MOBA_HEADS * MOBA_VROWS, MOBA_BLOCK), BF16),
                   jax.ShapeDtypeStruct((B, nb, MOBA_HEADS, 16, MOBA_BLOCK), F32)],
        scratch_shapes=[pltpu.VMEM((MOBA_HEADS * 16, MOBA_W), F32)],
        compiler_params=_params("arbitrary", "arbitrary"),
        name="moba_prep",
    )(proj, proj, proj, qw, kw, cos_t, sin_t)


def _moba_kernel(qt_ref, k_ref, vt_ref, bias_ref, o_ref, acc_ref, m_ref, s0_ref, s1_ref):
    j = pl.program_id(1)
    hd = MOBA_HD
    kpos = lax.broadcasted_iota(jnp.int32, (MOBA_BLOCK, MOBA_BLOCK), 0)
    qpos = lax.broadcasted_iota(jnp.int32, (MOBA_BLOCK, MOBA_BLOCK), 1)
    causal = kpos <= qpos
    own = pl.ds(pl.multiple_of(j * MOBA_BLOCK, MOBA_BLOCK), MOBA_BLOCK)
    heads = range(MOBA_HEADS)
    rows = [slice(h * hd, (h + 1) * hd) for h in heads]
    vrows = [slice(h * MOBA_VROWS, (h + 1) * MOBA_VROWS) for h in heads]
    s = [jnp.where(causal, jnp.dot(k_ref[0, h, own, :], qt_ref[0, rows[h], :], preferred_element_type=F32),
                   -jnp.inf) for h in heads]
    m = [jnp.max(s[h], axis=0, keepdims=True) for h in heads]
    p = [jnp.exp2(s[h] - m[h]) for h in heads]
    pv = [jnp.dot(vt_ref[0, j, vrows[h], :], p[h].astype(BF16), preferred_element_type=F32)
          for h in heads]
    for h in heads:
        m_ref[h] = m[h]
        acc_ref[h] = pv[h]

    nb = k_ref.shape[2] // MOBA_BLOCK

    def scores(n, s_ref):
        nc = jnp.minimum(n, nb - 1)
        blk = pl.ds(pl.multiple_of(nc * MOBA_BLOCK, MOBA_BLOCK), MOBA_BLOCK)
        for h in heads:
            s_ref[h] = (jnp.dot(k_ref[0, h, blk, :], qt_ref[0, rows[h], :], preferred_element_type=F32)
                        + bias_ref[0, 0, h, pl.ds(nc, 1), :])

    def consume(n, s_ref):
        s = {h: s_ref[h] for h in heads}
        m_old = {h: m_ref[h] for h in heads}
        m_new = {h: jnp.maximum(m_old[h], jnp.max(s[h], axis=0, keepdims=True)) for h in heads}
        alpha = {h: jnp.exp2(m_old[h] - m_new[h]) for h in heads}
        p = {h: jnp.exp2(s[h] - m_new[h]) for h in heads}
        pv = {h: jnp.dot(vt_ref[0, n, vrows[h], :], p[h].astype(BF16), preferred_element_type=F32)
              for h in heads}
        acc_new = {h: acc_ref[h] * alpha[h] + pv[h] for h in heads}
        for h in heads:
            m_ref[h] = m_new[h]
            acc_ref[h] = acc_new[h]

    scores(0, s0_ref)

    def pair(nn, carry):
        n = 2 * nn
        scores(n + 1, s1_ref)
        consume(n, s0_ref)
        scores(n + 2, s0_ref)
        consume(n + 1, s1_ref)
        return carry

    lax.fori_loop(0, j // 2, pair, 0)

    @pl.when(j % 2 == 1)
    def _():
        consume(j - 1, s0_ref)

    for h in range(MOBA_HEADS):
        o_ref[0, h * hd:(h + 1) * hd, :] = (acc_ref[h, 0:hd, :] / acc_ref[h, hd:hd + 1, :]).astype(BF16)


def _moba(qt, kh, vt, bias, B, S):
    nb = S // MOBA_BLOCK
    return pl.pallas_call(
        _moba_kernel,
        grid=(B, nb),
        in_specs=[pl.BlockSpec((1, MOBA_W, MOBA_BLOCK), lambda b, j: (b, 0, j)),
                  pl.BlockSpec((1, MOBA_HEADS, S, MOBA_HD), lambda b, j: (b, 0, 0, 0)),
                  pl.BlockSpec((1, nb, MOBA_HEADS * MOBA_VROWS, MOBA_BLOCK), lambda b, j: (b, 0, 0, 0)),
                  pl.BlockSpec((1, 1, MOBA_HEADS, 16, MOBA_BLOCK), lambda b, j: (b, j, 0, 0, 0))],
        out_specs=pl.BlockSpec((1, MOBA_W, MOBA_BLOCK), lambda b, j: (b, 0, j)),
        out_shape=jax.ShapeDtypeStruct((B, MOBA_W, S), BF16),
        scratch_shapes=[pltpu.VMEM((MOBA_HEADS, MOBA_VROWS, MOBA_BLOCK), F32),
                        pltpu.VMEM((MOBA_HEADS, 1, MOBA_BLOCK), F32),
                        pltpu.VMEM((MOBA_HEADS, MOBA_BLOCK, MOBA_BLOCK), F32),
                        pltpu.VMEM((MOBA_HEADS, MOBA_BLOCK, MOBA_BLOCK), F32)],
        compiler_params=_params("arbitrary", "arbitrary"),
        name="moba",
    )(qt, kh, vt, bias)


def _outproj_kernel(x_ref, og_ref, omt_ref, msm_ref, wo_ref, n2_ref, wq_ref, x1_ref, xn_ref, q_ref):
    om = (omt_ref[0].astype(F32).T * msm_ref[...]).astype(BF16)
    x1 = (x_ref[...]
          + jnp.dot(og_ref[...], wo_ref[0:GLA_V, :], preferred_element_type=F32)
          + jnp.dot(om, wo_ref[GLA_V:, :], preferred_element_type=F32))
    x1_ref[...] = x1
    ms = jnp.mean(x1 * x1, axis=-1, keepdims=True)
    xn = (x1 * lax.rsqrt(ms + EPS) * n2_ref[...]).astype(BF16)
    xn_ref[...] = xn
    q_ref[...] = jnp.dot(xn, wq_ref[...], preferred_element_type=F32).astype(BF16)


def _outproj(x2, o_gla, o_moba_t, ms_moba, w_out, norm2_w, wq, B, S):
    T = B * S
    nt = S // OUT_TILE
    QW = PEER_HEADS * PEER_QDIM
    return pl.pallas_call(
        _outproj_kernel,
        grid=(T // OUT_TILE,),
        in_specs=[pl.BlockSpec((OUT_TILE, D_MODEL), lambda i: (i, 0)),
                  pl.BlockSpec((OUT_TILE, GLA_V), lambda i: (i, 0)),
                  pl.BlockSpec((1, MOBA_W, OUT_TILE), lambda i: (i // nt, 0, i % nt)),
                  pl.BlockSpec((1, MOBA_W), lambda i: (0, 0)),
                  pl.BlockSpec((GLA_V + MOBA_W, D_MODEL), lambda i: (0, 0)),
                  pl.BlockSpec((1, D_MODEL), lambda i: (0, 0)),
                  pl.BlockSpec((D_MODEL, QW), lambda i: (0, 0))],
        out_specs=[pl.BlockSpec((OUT_TILE, D_MODEL), lambda i: (i, 0)),
                   pl.BlockSpec((OUT_TILE, D_MODEL), lambda i: (i, 0)),
                   pl.BlockSpec((OUT_TILE, QW), lambda i: (i, 0))],
        out_shape=[jax.ShapeDtypeStruct((T, D_MODEL), F32),
                   jax.ShapeDtypeStruct((T, D_MODEL), BF16),
                   jax.ShapeDtypeStruct((T, QW), BF16)],
        compiler_params=_params("arbitrary"),
        name="outproj",
    )(x2, o_gla, o_moba_t, ms_moba, w_out, norm2_w, wq)


_PAIRS = [(r, q) for r in range(PEER_TOPK + 1) for q in range(PEER_TOPK + 1)
          if (r + 1) * (q + 1) <= PEER_TOPK + 1]


def _tree(op, xs):
    xs = list(xs)
    while len(xs) > 1:
        xs = [op(xs[i], xs[i + 1]) if i + 1 < len(xs) else xs[i] for i in range(0, len(xs), 2)]
    return xs[0]


def _route_kernel(q_ref, sk_ref, rk_ref, e1_ref, cnt_ref, e0_ref, sc_ref, tmp_ref):
    H, NK = PEER_HEADS, PEER_NKEYS
    hq = H * PEER_QDIM // 2
    G = 8
    NTOP = PEER_TOPK + 1
    hd = PEER_QDIM // 2
    for p in range(2):
        for h in range(H):
            sc = _nt_dot(sk_ref[h, p], q_ref[:, (p * H + h) * hd:(p * H + h + 1) * hd])
            for lt in range(ROUTE_TILE // ROUTE_LANES):
                sc_ref[p, lt, pl.ds(h, NK, stride=H), :] = sc[:, lt * ROUTE_LANES:(lt + 1) * ROUTE_LANES]
    neg = jnp.full((H, ROUTE_LANES), -jnp.inf, F32)

    def insert(top, v):
        out = []
        for t in top:
            out.append(jnp.maximum(t, v))
            v = jnp.minimum(t, v)
        return out

    for lt in range(ROUTE_TILE // ROUTE_LANES):
        lanes = slice(lt * ROUTE_LANES, (lt + 1) * ROUTE_LANES)

        def largest(p, lt=lt):
            def body(g, top):
                v = sc_ref[p, lt, pl.ds(pl.multiple_of(g * (G * H), G * H), G * H), :]
                top = list(top)
                for k in range(G):
                    top = insert(top, v[k * H:(k + 1) * H])
                return tuple(top)
            return lax.fori_loop(0, NK // G, body, (neg,) * NTOP)

        tops = (largest(0), largest(1))
        best = [neg] * NTOP
        for r, q in _PAIRS:
            best = insert(best, tops[0][r] + tops[1][q])
        tau = 0.5 * (best[PEER_TOPK - 1] + best[PEER_TOPK])
        top = tops[0][0] + tops[1][0]
        zinv = 0.5 / _tree(jnp.add, [jnp.where(tops[0][r] + tops[1][q] > tau,
                                               jnp.exp(tops[0][r] + tops[1][q] - top), 0.0)
                                     for r, q in _PAIRS])
        thr = [tau - b for b in tops[1][:PEER_TOPK]]
        top1 = tops[1][:PEER_TOPK]

        def half0(g, carry, lt=lt, tops=tops, thr=thr, zinv=zinv):
            base = pl.multiple_of(g * (G * H), G * H)
            v0 = sc_ref[0, lt, pl.ds(base, G * H), :]
            for k in range(G):
                a = v0[k * H:(k + 1) * H]
                rows = pl.ds(pl.multiple_of(base + k * H, H), H)
                tmp_ref[0, rows, :] = _tree(jnp.add, [jnp.where(a > t, 1.0, 0.0) for t in thr])
                tmp_ref[1, rows, :] = jnp.exp(a - tops[0][0]) * zinv
            return carry

        def half1(g, carry, lt=lt, tops=tops, top1=top1):
            base = pl.multiple_of(g * (G * H), G * H)
            v1 = sc_ref[1, lt, pl.ds(base, G * H), :]
            for k in range(G):
                b = v1[k * H:(k + 1) * H]
                rows = pl.ds(pl.multiple_of(base + k * H, H), H)
                tmp_ref[2, rows, :] = 1.0 + _tree(jnp.add, [jnp.where(b < t, 1.0, 0.0) for t in top1])
                tmp_ref[3, rows, :] = jnp.exp(b - tops[1][0])
            return carry

        lax.fori_loop(0, NK // G, half0, 0)
        lax.fori_loop(0, NK // G, half1, 0)
        for h in range(H):
            cnt_ref[h, :, lanes] = tmp_ref[0, pl.ds(h, NK, stride=H), :]
            e0_ref[h, :, lanes] = tmp_ref[1, pl.ds(h, NK, stride=H), :]
            rk_ref[h, :, lanes] = tmp_ref[2, pl.ds(h, NK, stride=H), :].astype(BF16)
            e1_ref[h, :, lanes] = tmp_ref[3, pl.ds(h, NK, stride=H), :].astype(BF16)


def _route(q, subkeys):
    T = q.shape[0]
    H, NK = PEER_HEADS, PEER_NKEYS
    hq = H * PEER_QDIM // 2
    spec = pl.BlockSpec((H, NK, ROUTE_TILE), lambda i: (0, 0, i))
    return pl.pallas_call(
        _route_kernel,
        grid=(T // ROUTE_TILE,),
        in_specs=[pl.BlockSpec((ROUTE_TILE, 2 * hq), lambda i: (i, 0)),
                  pl.BlockSpec((H, 2, NK, PEER_QDIM // 2), lambda i: (0, 0, 0, 0))],
        out_specs=[spec, spec, spec, spec],
        out_shape=[jax.ShapeDtypeStruct((H, NK, T), BF16), jax.ShapeDtypeStruct((H, NK, T), BF16),
                   jax.ShapeDtypeStruct((H, NK, T), F32), jax.ShapeDtypeStruct((H, NK, T), F32)],
        scratch_shapes=[pltpu.VMEM((2, ROUTE_TILE // ROUTE_LANES, NK * H, ROUTE_LANES), F32),
                        pltpu.VMEM((4, NK * H, ROUTE_LANES), F32)],
        compiler_params=_params("arbitrary"),
        name="peer_route",
    )(q, subkeys)


PEER_NCHUNK = PEER_N // PEER_ECHUNK


def _peer_kernel(xn_ref, x1_ref, u_ref, vt_ref, rk_ref, e1_ref, cnt_ref, e0_ref, o_ref,
                 acc_ref, act_ref, w_ref):
    c = pl.program_id(1)
    NK = PEER_NKEYS
    nsub = PEER_ECHUNK // NK
    RB = PEER_STRIP_ROWS

    @pl.when(c == 0)
    def _():
        acc_ref[...] = jnp.zeros_like(acc_ref)

    bounds = (0,) + PEER_ACT_SPLITS + (PEER_ECHUNK,)
    for ii in range(nsub):
        if ii * NK in bounds[:-1]:
            hs = slice(ii * NK, bounds[bounds.index(ii * NK) + 1])
            act_ref[hs, :] = _nt_dot(u_ref[hs, :], xn_ref[...])
        i = c * nsub + ii
        cnt_rows = [cnt_ref[h, pl.ds(i, 1), :] for h in range(PEER_HEADS)]
        e0_rows = [e0_ref[h, pl.ds(i, 1), :] for h in range(PEER_HEADS)]
        for lc in range(PEER_TILE // LANES):
            lanes = slice(lc * LANES, (lc + 1) * LANES)
            cnts = [jnp.broadcast_to(r[:, lanes], (RB, LANES)).astype(BF16) for r in cnt_rows]
            e0s = [jnp.broadcast_to(r[:, lanes], (RB, LANES)).astype(BF16) for r in e0_rows]
            for rb in range(NK // RB):
                rows = slice(rb * RB, (rb + 1) * RB)
                wsum = _tree(jnp.add, [
                    e0s[h] * jnp.where(rk_ref[h, rows, lanes] <= cnts[h], e1_ref[h, rows, lanes], 0.0)
                    for h in range(PEER_HEADS)])
                er = slice(ii * NK + rb * RB, ii * NK + (rb + 1) * RB)
                a = act_ref[er, lanes]
                g2 = a * (1.0 + lax.erf(a * (1.0 / math.sqrt(2.0))))
                w_ref[er, lanes] = g2.astype(BF16) * wsum
    acc_ref[...] += jnp.dot(vt_ref[0], w_ref[...], preferred_element_type=F32)

    @pl.when(c == PEER_NCHUNK - 1)
    def _():
        o_ref[...] = x1_ref[...] + acc_ref[...].T


def _peer(xn, x1, u, vt, rk, e1, cnt, e0):
    T = xn.shape[0]
    H, NK = PEER_HEADS, PEER_NKEYS
    rspec = pl.BlockSpec((H, NK, PEER_TILE), lambda t, c: (0, 0, t))
    return pl.pallas_call(
        _peer_kernel,
        grid=(T // PEER_TILE, PEER_NCHUNK),
        in_specs=[pl.BlockSpec((PEER_TILE, D_MODEL), lambda t, c: (t, 0)),
                  pl.BlockSpec((PEER_TILE, D_MODEL), lambda t, c: (t, 0)),
                  pl.BlockSpec((PEER_ECHUNK, D_MODEL), lambda t, c: (c, 0)),
                  pl.BlockSpec((1, D_MODEL, PEER_ECHUNK), lambda t, c: (c, 0, 0)),
                  rspec, rspec, rspec, rspec],
        out_specs=pl.BlockSpec((PEER_TILE, D_MODEL), lambda t, c: (t, 0)),
        out_shape=jax.ShapeDtypeStruct((T, D_MODEL), F32),
        scratch_shapes=[pltpu.VMEM((D_MODEL, PEER_TILE), F32),
                        pltpu.VMEM((PEER_ECHUNK, PEER_TILE), F32),
                        pltpu.VMEM((PEER_ECHUNK, PEER_TILE), BF16)],
        compiler_params=_params("arbitrary", "arbitrary"),
        name="peer",
    )(xn, x1, u, vt, rk, e1, cnt, e0)


def _layer(x, norm1_w, w_in, w_alpha, b_alpha, out_norm_w, qn_w, kn_w, mix_scale, w_out,
           norm2_w, w_query, subkeys, u_tab, v_tab):
    B, S, D = x.shape
    T = B * S
    x2 = x.reshape(T, D)

    o_gr = 2 * GLA_QK + 2 * GLA_V
    w_r = jnp.concatenate([w_in[:, :o_gr], w_in[:, o_gr + GLA_RANK:], w_in[:, o_gr:o_gr + GLA_RANK],
                           jnp.zeros((D, 128 - GLA_RANK), w_in.dtype)], axis=1).astype(BF16)
    wa = jnp.concatenate([w_alpha, jnp.zeros((128 - GLA_RANK, GLA_QK), w_alpha.dtype)], axis=0)

    proj = _inproj(x2, norm1_w.reshape(1, D), w_r)
    o_gla = _gla(proj, B, S, wa, b_alpha.reshape(1, GLA_QK), out_norm_w.reshape(1, GLA_DV),
                 mix_scale[:GLA_V].reshape(1, GLA_V))

    half = MOBA_HD // 2
    inv = ROPE_THETA ** (-jnp.arange(half, dtype=F32) / half)
    ang = inv[:, None] * jnp.arange(S, dtype=F32)[None, :]
    qw = jnp.broadcast_to(qn_w.reshape(MOBA_HD, 1), (MOBA_HD, MOBA_BLOCK))
    kw = jnp.broadcast_to(kn_w.reshape(MOBA_HD, 1), (MOBA_HD, MOBA_BLOCK))
    qt, kh, vt, bias = _moba_prep(proj, B, S, qw, kw, jnp.cos(ang), jnp.sin(ang))
    o_moba_t = _moba(qt, kh, vt, bias, B, S)

    H, hd = PEER_HEADS, PEER_QDIM // 2
    wq = w_query.reshape(D, H, 2, hd).transpose(0, 2, 1, 3).reshape(D, 2 * H * hd).astype(BF16)

    x1, xn, q = _outproj(x2, o_gla, o_moba_t, mix_scale[GLA_V:].reshape(1, MOBA_W),
                         w_out.astype(BF16), norm2_w.reshape(1, D), wq, B, S)
    rk, e1, cnt, e0 = _route(q, subkeys.astype(BF16))
    vt = v_tab.reshape(PEER_NCHUNK, PEER_ECHUNK, D).transpose(0, 2, 1).astype(BF16)
    out = _peer(xn, x1, u_tab.astype(BF16), vt, rk, e1, cnt, e0)
    return out.reshape(B, S, D)


def kernel(x, norm1_w, w_in, gla_w_alpha, gla_b_alpha, gla_out_norm_w, moba_q_norm_w, moba_k_norm_w,
           mix_scale, w_out, norm2_w, peer_w_query, peer_subkeys, peer_u, peer_v):
    assert norm1_w.shape[0] == 1, "single-layer kernel"
    return _layer(x, norm1_w[0], w_in[0], gla_w_alpha[0], gla_b_alpha[0], gla_out_norm_w[0],
                  moba_q_norm_w[0], moba_k_norm_w[0], mix_scale[0], w_out[0], norm2_w[0],
                  peer_w_query[0], peer_subkeys[0], peer_u[0], peer_v[0])
```

```python
import math

import jax
import jax.numpy as jnp
from jax import lax
from jax.experimental import pallas as pl
from jax.experimental.pallas import tpu as pltpu

D_MODEL = 1024
GLA_HEADS, GLA_DK, GLA_DV, GLA_RANK, GLA_TAU, GLA_CHUNK = 4, 64, 128, 16, 16.0, 64
MOBA_HEADS, MOBA_HD, MOBA_BLOCK, MOBA_TOPK = 8, 64, 256, 3
ROPE_THETA = 10000.0
PEER_HEADS, PEER_NKEYS, PEER_QDIM, PEER_TOPK = 8, 128, 256, 16
PEER_N = PEER_NKEYS * PEER_NKEYS
EPS = 1e-6

GLA_QK = GLA_HEADS * GLA_DK
GLA_V = GLA_HEADS * GLA_DV
MOBA_W = MOBA_HEADS * MOBA_HD
PROJ_W = 2 * GLA_QK + GLA_V + 2 * MOBA_W + 128
GR_COLBLOCK = (PROJ_W - 128) // 128
PROJV_W = GLA_V + MOBA_W

F32 = jnp.float32
BF16 = jnp.bfloat16
HIGHEST = lax.Precision.HIGHEST
NEG = -1e30
LOG2E = math.log2(math.e)

VMEM_LIMIT_BYTES = 56 * 1024 * 1024

IN_TILE = 512
GLA_TILE = 256
MOBA_VROWS = MOBA_HD + 16
OUT_TILE = 256
ROUTE_TILE = 256
ROUTE_LANES = 128
PEER_TILE = 512
PEER_ECHUNK = 2048
PEER_ACT_SPLITS = (1024,)
PEER_STRIP_ROWS = 32
LANES = 256


def _params(*sem):
    return pltpu.CompilerParams(dimension_semantics=sem, vmem_limit_bytes=VMEM_LIMIT_BYTES)


def _nt_dot(a, b, precision=None):
    return lax.dot_general(a, b, (((1,), (1,)), ((), ())), precision=precision,
                           preferred_element_type=F32)


def _inproj_kernel(x_ref, nw_ref, w_ref, wv_ref, o_ref, ov_ref):
    x = x_ref[...]
    ms = jnp.mean(x * x, axis=-1, keepdims=True)
    xn = (x * lax.rsqrt(ms + EPS) * nw_ref[...]).astype(BF16)
    o_ref[...] = jnp.dot(xn, w_ref[...], preferred_element_type=F32)
    ov_ref[...] = jnp.dot(xn, wv_ref[...], preferred_element_type=F32).astype(BF16)


def _inproj(x2, norm_w, w, wv):
    T = x2.shape[0]
    return pl.pallas_call(
        _inproj_kernel,
        grid=(T // IN_TILE,),
        in_specs=[pl.BlockSpec((IN_TILE, D_MODEL), lambda i: (i, 0)),
                  pl.BlockSpec((1, D_MODEL), lambda i: (0, 0)),
                  pl.BlockSpec((D_MODEL, PROJ_W), lambda i: (0, 0)),
                  pl.BlockSpec((D_MODEL, PROJV_W), lambda i: (0, 0))],
        out_specs=[pl.BlockSpec((IN_TILE, PROJ_W), lambda i: (i, 0)),
                   pl.BlockSpec((IN_TILE, PROJV_W), lambda i: (i, 0))],
        out_shape=[jax.ShapeDtypeStruct((T, PROJ_W), F32),
                   jax.ShapeDtypeStruct((T, PROJV_W), BF16)],
        compiler_params=_params("arbitrary"),
        name="inproj",
    )(x2, norm_w, w, wv)


def _gla_kernel(qk_ref, v_ref, gate_ref, gr_ref, wa_ref, ba_ref, nw_ref, ms_ref, o_ref, st_ref):
    @pl.when(pl.program_id(1) == 0)
    def _():
        st_ref[...] = jnp.zeros_like(st_ref)

    C = GLA_CHUNK
    z = jnp.dot(gr_ref[...], wa_ref[...], precision=HIGHEST, preferred_element_type=F32) + ba_ref[...]
    log_a = (jnp.minimum(z, 0.0) - jnp.log1p(jnp.exp(-jnp.abs(z)))) * (1.0 / GLA_TAU)
    row = lax.broadcasted_iota(jnp.int32, (C, C), 0)
    col = lax.broadcasted_iota(jnp.int32, (C, C), 1)
    causal = col <= row
    tril = causal.astype(F32)
    chunks = range(GLA_TILE // C)
    heads = range(GLA_HEADS)
    ks = [slice(h * GLA_DK, (h + 1) * GLA_DK) for h in heads]
    vs = [slice(h * GLA_DV, (h + 1) * GLA_DV) for h in heads]
    sl = [slice(c * C, (c + 1) * C) for c in chunks]
    q_d, k_in, k_st, decay = [], [], [], []
    for c in chunks:
        b = jnp.dot(tril, log_a[sl[c]], precision=HIGHEST, preferred_element_type=F32)
        b_last = b[C - 1:C, :]
        q = qk_ref[sl[c], 0:GLA_QK]
        k = qk_ref[sl[c], GLA_QK:2 * GLA_QK]
        q_d.append(q * jnp.exp(b) * (GLA_DK ** -0.5))
        k_in.append(k * jnp.exp(-b))
        k_st.append((k * jnp.exp(b_last - b)).astype(BF16))
        decay.append(jnp.exp(b_last))
    v = [[v_ref[sl[c], vs[h]] for h in heads] for c in chunks]
    a = [[jnp.where(causal, _nt_dot(q_d[c][:, ks[h]], k_in[c][:, ks[h]], precision=HIGHEST), 0.0)
          for h in heads] for c in chunks]
    o = [[jnp.dot(a[c][h].astype(BF16), v[c][h], preferred_element_type=F32)
          for h in heads] for c in chunks]
    upd = [[jnp.dot(v[c][h].astype(F32).T.astype(BF16), k_st[c][:, ks[h]], preferred_element_type=F32)
            for h in heads] for c in chunks]
    st = [st_ref[h] for h in heads]
    for c in chunks:
        for h in heads:
            o[c][h] = o[c][h] + _nt_dot(q_d[c][:, ks[h]].astype(BF16), st[h].astype(BF16))
            st[h] = st[h] * decay[c][:, ks[h]] + upd[c][h]
    out = []
    for c in chunks:
        for h in heads:
            y = o[c][h]
            y = y * lax.rsqrt(jnp.mean(y * y, axis=-1, keepdims=True) + EPS) * nw_ref[...]
            g = gate_ref[sl[c], vs[h]]
            out.append((y * (g * jax.nn.sigmoid(g)) * ms_ref[:, vs[h]]).astype(BF16))
    for c in chunks:
        for h in heads:
            o_ref[sl[c], vs[h]] = out[c * GLA_HEADS + h]
    for h in heads:
        st_ref[h] = st[h]


def _gla(proj, projv, B, S, w_alpha, b_alpha, out_norm_w, mix_scale_gla):
    nt = S // GLA_TILE
    tok = lambda b, j: b * nt + j
    return pl.pallas_call(
        _gla_kernel,
        grid=(B, nt),
        in_specs=[pl.BlockSpec((GLA_TILE, 2 * GLA_QK), lambda b, j: (tok(b, j), 0)),
                  pl.BlockSpec((GLA_TILE, GLA_V), lambda b, j: (tok(b, j), 0)),
                  pl.BlockSpec((GLA_TILE, GLA_V), lambda b, j: (tok(b, j), 1)),
                  pl.BlockSpec((GLA_TILE, 128), lambda b, j: (tok(b, j), GR_COLBLOCK)),
                  pl.BlockSpec((128, GLA_QK), lambda b, j: (0, 0)),
                  pl.BlockSpec((1, GLA_QK), lambda b, j: (0, 0)),
                  pl.BlockSpec((1, GLA_DV), lambda b, j: (0, 0)),
                  pl.BlockSpec((1, GLA_V), lambda b, j: (0, 0))],
        out_specs=pl.BlockSpec((GLA_TILE, GLA_V), lambda b, j: (tok(b, j), 0)),
        out_shape=jax.ShapeDtypeStruct((B * S, GLA_V), BF16),
        scratch_shapes=[pltpu.VMEM((GLA_HEADS, GLA_DV, GLA_DK), F32)],
        compiler_params=_params("arbitrary", "arbitrary"),
        name="gla",
    )(proj, projv, proj, proj, w_alpha, b_alpha, out_norm_w, mix_scale_gla)


def _norm_rope_t(xt, w_ref, cos_ref, sin_ref):
    half = MOBA_HD // 2
    outs = []
    for h in range(MOBA_HEADS):
        xh = xt[h * MOBA_HD:(h + 1) * MOBA_HD]
        ms = jnp.mean(xh * xh, axis=0, keepdims=True)
        y = xh * lax.rsqrt(ms + EPS) * w_ref[...]
        y1, y2 = y[:half], y[half:]
        c, s = cos_ref[...], sin_ref[...]
        outs.append(y1 * c - y2 * s)
        outs.append(y2 * c + y1 * s)
    return jnp.concatenate(outs, axis=0)


def _moba_prep_kernel(q_ref, k_ref, v_ref, qw_ref, kw_ref, cos_ref, sin_ref,
                      qt_ref, kh_ref, vt_ref, bias_ref, mt_ref):
    j = pl.program_id(1)
    nb = 16

    @pl.when(j == 0)
    def _():
        mt_ref[...] = jnp.zeros_like(mt_ref)

    qt = _norm_rope_t(q_ref[...].T, qw_ref, cos_ref, sin_ref)
    kt = _norm_rope_t(k_ref[...].T, kw_ref, cos_ref, sin_ref)
    qt_ref[0] = (qt * (MOBA_HD ** -0.5 * LOG2E)).astype(BF16)
    kn = kt.T
    for h in range(MOBA_HEADS):
        kh_ref[0, h] = kn[:, h * MOBA_HD:(h + 1) * MOBA_HD].astype(BF16)
    vtt = v_ref[...].astype(F32).T
    ones_tile = (lax.broadcasted_iota(jnp.int32, (MOBA_VROWS - MOBA_HD, MOBA_BLOCK), 0) == 0).astype(BF16)
    for h in range(MOBA_HEADS):
        vt_ref[0, 0, h * MOBA_VROWS:h * MOBA_VROWS + MOBA_HD] = vtt[h * MOBA_HD:(h + 1) * MOBA_HD].astype(BF16)
        vt_ref[0, 0, h * MOBA_VROWS + MOBA_HD:(h + 1) * MOBA_VROWS] = ones_tile

    bs = jnp.dot(mt_ref[...], qt, precision=HIGHEST, preferred_element_type=F32)
    n_idx = lax.broadcasted_iota(jnp.int32, (nb, MOBA_BLOCK), 0)
    for h in range(MOBA_HEADS):
        sc = jnp.where(n_idx < j, bs[h * nb:(h + 1) * nb], -jnp.inf)
        sel = jnp.zeros((nb, MOBA_BLOCK), jnp.bool_)
        for r in range(MOBA_TOPK):
            m = jnp.max(sc, axis=0, keepdims=True)
            first = jnp.min(jnp.where(sc == m, n_idx, nb), axis=0, keepdims=True)
            hit = n_idx == first
            sel = jnp.logical_or(sel, jnp.logical_and(hit, r < j))
            sc = jnp.where(hit, -jnp.inf, sc)
        bias_ref[0, 0, h] = jnp.where(sel, 0.0, NEG)

    kbar = jnp.mean(kn, axis=0, keepdims=True)
    r_idx = lax.broadcasted_iota(jnp.int32, (MOBA_HEADS * nb, MOBA_W), 0)
    c_idx = lax.broadcasted_iota(jnp.int32, (MOBA_HEADS * nb, MOBA_W), 1)
    mine = jnp.logical_and(r_idx % nb == j, r_idx // nb == c_idx // MOBA_HD)
    mt_ref[...] = jnp.where(mine, kbar, mt_ref[...])


def _moba_prep(proj, projv, B, S, qw, kw, cos_t, sin_t):
    nb = S // MOBA_BLOCK
    assert nb <= 16
    tok = lambda b, j: b * nb + j
    half = MOBA_HD // 2
    return pl.pallas_call(
        _moba_prep_kernel,
        grid=(B, nb),
        in_specs=[pl.BlockSpec((MOBA_BLOCK, MOBA_W), lambda b, j: (tok(b, j), 2)),
                  pl.BlockSpec((MOBA_BLOCK, MOBA_W), lambda b, j: (tok(b, j), 3)),
                  pl.BlockSpec((MOBA_BLOCK, MOBA_W), lambda b, j: (tok(b, j), 1)),
                  pl.BlockSpec((MOBA_HD, MOBA_BLOCK), lambda b, j: (0, 0)),
                  pl.BlockSpec((MOBA_HD, MOBA_BLOCK), lambda b, j: (0, 0)),
                  pl.BlockSpec((half, MOBA_BLOCK), lambda b, j: (0, j)),
                  pl.BlockSpec((half, MOBA_BLOCK), lambda b, j: (0, j))],
        out_specs=[pl.BlockSpec((1, MOBA_W, MOBA_BLOCK), lambda b, j: (b, 0, j)),
                   pl.BlockSpec((1, MOBA_HEADS, MOBA_BLOCK, MOBA_HD), lambda b, j: (b, 0, j, 0)),
                   pl.BlockSpec((1, 1, MOBA_HEADS * MOBA_VROWS, MOBA_BLOCK), lambda b, j: (b, j, 0, 0)),
                   pl.BlockSpec((1, 1, MOBA_HEADS, 16, MOBA_BLOCK), lambda b, j: (b, j, 0, 0, 0))],
        out_shape=[jax.ShapeDtypeStruct((B, MOBA_W, S), BF16),
                   jax.ShapeDtypeStruct((B, MOBA_HEADS, S, MOBA_HD), BF16),
                   jax.ShapeDtypeStruct((B, nb, MOBA_HEADS * MOBA_VROWS, MOBA_BLOCK), BF16),
                   jax.ShapeDtypeStruct((B, nb, MOBA_HEADS, 16, MOBA_BLOCK), F32)],
        scratch_shapes=[pltpu.VMEM((MOBA_HEADS * 16, MOBA_W), F32)],
        compiler_params=_params("arbitrary", "arbitrary"),
        name="moba_prep",
    )(proj, proj, projv, qw, kw, cos_t, sin_t)


def _moba_kernel(qt_ref, k_ref, vt_ref, bias_ref, o_ref, acc_ref, m_ref, s0_ref, s1_ref):
    j = pl.program_id(1)
    hd = MOBA_HD
    kpos = lax.broadcasted_iota(jnp.int32, (MOBA_BLOCK, MOBA_BLOCK), 0)
    qpos = lax.broadcasted_iota(jnp.int32, (MOBA_BLOCK, MOBA_BLOCK), 1)
    causal = kpos <= qpos
    own = pl.ds(pl.multiple_of(j * MOBA_BLOCK, MOBA_BLOCK), MOBA_BLOCK)
    heads = range(MOBA_HEADS)
    rows = [slice(h * hd, (h + 1) * hd) for h in heads]
    vrows = [slice(h * MOBA_VROWS, (h + 1) * MOBA_VROWS) for h in heads]
    s = [jnp.where(causal, jnp.dot(k_ref[0, h, own, :], qt_ref[0, rows[h], :], preferred_element_type=F32),
                   -jnp.inf) for h in heads]
    m = [jnp.max(s[h], axis=0, keepdims=True) for h in heads]
    p = [jnp.exp2(s[h] - m[h]) for h in heads]
    pv = [jnp.dot(vt_ref[0, j, vrows[h], :], p[h].astype(BF16), preferred_element_type=F32)
          for h in heads]
    for h in heads:
        m_ref[h] = m[h]
        acc_ref[h] = pv[h]

    nb = k_ref.shape[2] // MOBA_BLOCK

    def scores(n, s_ref):
        nc = jnp.minimum(n, nb - 1)
        blk = pl.ds(pl.multiple_of(nc * MOBA_BLOCK, MOBA_BLOCK), MOBA_BLOCK)
        for h in heads:
            s_ref[h] = (jnp.dot(k_ref[0, h, blk, :], qt_ref[0, rows[h], :], preferred_element_type=F32)
                        + bias_ref[0, 0, h, pl.ds(nc, 1), :])

    def consume(n, s_ref):
        s = {h: s_ref[h] for h in heads}
        m_old = {h: m_ref[h] for h in heads}
        m_new = {h: jnp.maximum(m_old[h], jnp.max(s[h], axis=0, keepdims=True)) for h in heads}
        alpha = {h: jnp.exp2(m_old[h] - m_new[h]) for h in heads}
        p = {h: jnp.exp2(s[h] - m_new[h]) for h in heads}
        pv = {h: jnp.dot(vt_ref[0, n, vrows[h], :], p[h].astype(BF16), preferred_element_type=F32)
              for h in heads}
        acc_new = {h: acc_ref[h] * alpha[h] + pv[h] for h in heads}
        for h in heads:
            m_ref[h] = m_new[h]
            acc_ref[h] = acc_new[h]

    scores(0, s0_ref)

    def pair(nn, carry):
        n = 2 * nn
        scores(n + 1, s1_ref)
        consume(n, s0_ref)
        scores(n + 2, s0_ref)
        consume(n + 1, s1_ref)
        return carry

    lax.fori_loop(0, j // 2, pair, 0)

    @pl.when(j % 2 == 1)
    def _():
        consume(j - 1, s0_ref)

    for h in range(MOBA_HEADS):
        o_ref[0, h * hd:(h + 1) * hd, :] = (acc_ref[h, 0:hd, :] / acc_ref[h, hd:hd + 1, :]).astype(BF16)


def _moba(qt, kh, vt, bias, B, S):
    nb = S // MOBA_BLOCK
    return pl.pallas_call(
        _moba_kernel,
        grid=(B, nb),
        in_specs=[pl.BlockSpec((1, MOBA_W, MOBA_BLOCK), lambda b, j: (b, 0, j)),
                  pl.BlockSpec((1, MOBA_HEADS, S, MOBA_HD), lambda b, j: (b, 0, 0, 0)),
                  pl.BlockSpec((1, nb, MOBA_HEADS * MOBA_VROWS, MOBA_BLOCK), lambda b, j: (b, 0, 0, 0)),
                  pl.BlockSpec((1, 1, MOBA_HEADS, 16, MOBA_BLOCK), lambda b, j: (b, j, 0, 0, 0))],
        out_specs=pl.BlockSpec((1, MOBA_W, MOBA_BLOCK), lambda b, j: (b, 0, j)),
        out_shape=jax.ShapeDtypeStruct((B, MOBA_W, S), BF16),
        scratch_shapes=[pltpu.VMEM((MOBA_HEADS, MOBA_VROWS, MOBA_BLOCK), F32),
                        pltpu.VMEM((MOBA_HEADS, 1, MOBA_BLOCK), F32),
                        pltpu.VMEM((MOBA_HEADS, MOBA_BLOCK, MOBA_BLOCK), F32),
                        pltpu.VMEM((MOBA_HEADS, MOBA_BLOCK, MOBA_BLOCK), F32)],
        compiler_params=_params("arbitrary", "arbitrary"),
        name="moba",
    )(qt, kh, vt, bias)


def _outproj_kernel(x_ref, og_ref, omt_ref, msm_ref, wo_ref, n2_ref, wq_ref, x1_ref, xn_ref, q_ref):
    om = (omt_ref[0].astype(F32).T * msm_ref[...]).astype(BF16)
    x1 = (x_ref[...]
          + jnp.dot(og_ref[...], wo_ref[0:GLA_V, :], preferred_element_type=F32)
          + jnp.dot(om, wo_ref[GLA_V:, :], preferred_element_type=F32))
    x1_ref[...] = x1
    ms = jnp.mean(x1 * x1, axis=-1, keepdims=True)
    xn = (x1 * lax.rsqrt(ms + EPS) * n2_ref[...]).astype(BF16)
    xn_ref[...] = xn
    q_ref[...] = jnp.dot(xn, wq_ref[...], preferred_element_type=F32).astype(BF16)


def _outproj(x2, o_gla, o_moba_t, ms_moba, w_out, norm2_w, wq, B, S):
    T = B * S
    nt = S // OUT_TILE
    QW = PEER_HEADS * PEER_QDIM
    return pl.pallas_call(
        _outproj_kernel,
        grid=(T // OUT_TILE,),
        in_specs=[pl.BlockSpec((OUT_TILE, D_MODEL), lambda i: (i, 0)),
                  pl.BlockSpec((OUT_TILE, GLA_V), lambda i: (i, 0)),
                  pl.BlockSpec((1, MOBA_W, OUT_TILE), lambda i: (i // nt, 0, i % nt)),
                  pl.BlockSpec((1, MOBA_W), lambda i: (0, 0)),
                  pl.BlockSpec((GLA_V + MOBA_W, D_MODEL), lambda i: (0, 0)),
                  pl.BlockSpec((1, D_MODEL), lambda i: (0, 0)),
                  pl.BlockSpec((D_MODEL, QW), lambda i: (0, 0))],
        out_specs=[pl.BlockSpec((OUT_TILE, D_MODEL), lambda i: (i, 0)),
                   pl.BlockSpec((OUT_TILE, D_MODEL), lambda i: (i, 0)),
                   pl.BlockSpec((OUT_TILE, QW), lambda i: (i, 0))],
        out_shape=[jax.ShapeDtypeStruct((T, D_MODEL), F32),
                   jax.ShapeDtypeStruct((T, D_MODEL), BF16),
                   jax.ShapeDtypeStruct((T, QW), BF16)],
        compiler_params=_params("arbitrary"),
        name="outproj",
    )(x2, o_gla, o_moba_t, ms_moba, w_out, norm2_w, wq)


_PAIRS = [(r, q) for r in range(PEER_TOPK + 1) for q in range(PEER_TOPK + 1)
          if (r + 1) * (q + 1) <= PEER_TOPK + 1]


def _tree(op, xs):
    xs = list(xs)
    while len(xs) > 1:
        xs = [op(xs[i], xs[i + 1]) if i + 1 < len(xs) else xs[i] for i in range(0, len(xs), 2)]
    return xs[0]


def _route_kernel(q_ref, sk_ref, rk_ref, e1_ref, cnt_ref, e0_ref, sc_ref, tmp_ref):
    H, NK = PEER_HEADS, PEER_NKEYS
    hq = H * PEER_QDIM // 2
    G = 8
    NTOP = PEER_TOPK + 1
    hd = PEER_QDIM // 2
    for p in range(2):
        for h in range(H):
            sc = _nt_dot(sk_ref[h, p], q_ref[:, (p * H + h) * hd:(p * H + h + 1) * hd])
            for lt in range(ROUTE_TILE // ROUTE_LANES):
                sc_ref[p, lt, pl.ds(h, NK, stride=H), :] = sc[:, lt * ROUTE_LANES:(lt + 1) * ROUTE_LANES]
    neg = jnp.full((H, ROUTE_LANES), -jnp.inf, F32)

    def insert(top, v):
        out = []
        for t in top:
            out.append(jnp.maximum(t, v))
            v = jnp.minimum(t, v)
        return out

    for lt in range(ROUTE_TILE // ROUTE_LANES):
        lanes = slice(lt * ROUTE_LANES, (lt + 1) * ROUTE_LANES)

        def largest(p, lt=lt):
            def body(g, top):
                v = sc_ref[p, lt, pl.ds(pl.multiple_of(g * (G * H), G * H), G * H), :]
                top = list(top)
                for k in range(G):
                    top = insert(top, v[k * H:(k + 1) * H])
                return tuple(top)
            return lax.fori_loop(0, NK // G, body, (neg,) * NTOP)

        tops = (largest(0), largest(1))
        best = [neg] * NTOP
        for r, q in _PAIRS:
            best = insert(best, tops[0][r] + tops[1][q])
        tau = 0.5 * (best[PEER_TOPK - 1] + best[PEER_TOPK])
        top = tops[0][0] + tops[1][0]
        zinv = 0.5 / _tree(jnp.add, [jnp.where(tops[0][r] + tops[1][q] > tau,
                                               jnp.exp(tops[0][r] + tops[1][q] - top), 0.0)
                                     for r, q in _PAIRS])
        thr = [tau - b for b in tops[1][:PEER_TOPK]]
        top1 = tops[1][:PEER_TOPK]

        def half0(g, carry, lt=lt, tops=tops, thr=thr, zinv=zinv):
            base = pl.multiple_of(g * (G * H), G * H)
            v0 = sc_ref[0, lt, pl.ds(base, G * H), :]
            for k in range(G):
                a = v0[k * H:(k + 1) * H]
                rows = pl.ds(pl.multiple_of(base + k * H, H), H)
                tmp_ref[0, rows, :] = _tree(jnp.add, [jnp.where(a > t, 1.0, 0.0) for t in thr])
                tmp_ref[1, rows, :] = jnp.exp(a - tops[0][0]) * zinv
            return carry

        def half1(g, carry, lt=lt, tops=tops, top1=top1):
            base = pl.multiple_of(g * (G * H), G * H)
            v1 = sc_ref[1, lt, pl.ds(base, G * H), :]
            for k in range(G):
                b = v1[k * H:(k + 1) * H]
                rows = pl.ds(pl.multiple_of(base + k * H, H), H)
                tmp_ref[2, rows, :] = 1.0 + _tree(jnp.add, [jnp.where(b < t, 1.0, 0.0) for t in top1])
                tmp_ref[3, rows, :] = jnp.exp(b - tops[1][0])
            return carry

        lax.fori_loop(0, NK // G, half0, 0)
        lax.fori_loop(0, NK // G, half1, 0)
        for h in range(H):
            cnt_ref[h, :, lanes] = tmp_ref[0, pl.ds(h, NK, stride=H), :]
            e0_ref[h, :, lanes] = tmp_ref[1, pl.ds(h, NK, stride=H), :]
            rk_ref[h, :, lanes] = tmp_ref[2, pl.ds(h, NK, stride=H), :].astype(BF16)
            e1_ref[h, :, lanes] = tmp_ref[3, pl.ds(h, NK, stride=H), :].astype(BF16)


def _route(q, subkeys):
    T = q.shape[0]
    H, NK = PEER_HEADS, PEER_NKEYS
    hq = H * PEER_QDIM // 2
    spec = pl.BlockSpec((H, NK, ROUTE_TILE), lambda i: (0, 0, i))
    return pl.pallas_call(
        _route_kernel,
        grid=(T // ROUTE_TILE,),
        in_specs=[pl.BlockSpec((ROUTE_TILE, 2 * hq), lambda i: (i, 0)),
                  pl.BlockSpec((H, 2, NK, PEER_QDIM // 2), lambda i: (0, 0, 0, 0))],
        out_specs=[spec, spec, spec, spec],
        out_shape=[jax.ShapeDtypeStruct((H, NK, T), BF16), jax.ShapeDtypeStruct((H, NK, T), BF16),
                   jax.ShapeDtypeStruct((H, NK, T), F32), jax.ShapeDtypeStruct((H, NK, T), F32)],
        scratch_shapes=[pltpu.VMEM((2, ROUTE_TILE // ROUTE_LANES, NK * H, ROUTE_LANES), F32),
                        pltpu.VMEM((4, NK * H, ROUTE_LANES), F32)],
        compiler_params=_params("arbitrary"),
        name="peer_route",
    )(q, subkeys)


PEER_NCHUNK = PEER_N // PEER_ECHUNK


def _peer_kernel(xn_ref, x1_ref, u_ref, vt_ref, rk_ref, e1_ref, cnt_ref, e0_ref, o_ref,
                 acc_ref, act_ref, w_ref):
    c = pl.program_id(1)
    NK = PEER_NKEYS
    nsub = PEER_ECHUNK // NK
    RB = PEER_STRIP_ROWS

    @pl.when(c == 0)
    def _():
        acc_ref[...] = jnp.zeros_like(acc_ref)

    bounds = (0,) + PEER_ACT_SPLITS + (PEER_ECHUNK,)
    for ii in range(nsub):
        if ii * NK in bounds[:-1]:
            hs = slice(ii * NK, bounds[bounds.index(ii * NK) + 1])
            act_ref[hs, :] = _nt_dot(u_ref[hs, :], xn_ref[...])
        i = c * nsub + ii
        cnt_rows = [cnt_ref[h, pl.ds(i, 1), :] for h in range(PEER_HEADS)]
        e0_rows = [e0_ref[h, pl.ds(i, 1), :] for h in range(PEER_HEADS)]
        for lc in range(PEER_TILE // LANES):
            lanes = slice(lc * LANES, (lc + 1) * LANES)
            cnts = [jnp.broadcast_to(r[:, lanes], (RB, LANES)).astype(BF16) for r in cnt_rows]
            e0s = [jnp.broadcast_to(r[:, lanes], (RB, LANES)).astype(BF16) for r in e0_rows]
            for rb in range(NK // RB):
                rows = slice(rb * RB, (rb + 1) * RB)
                wsum = _tree(jnp.add, [
                    e0s[h] * jnp.where(rk_ref[h, rows, lanes] <= cnts[h], e1_ref[h, rows, lanes], 0.0)
                    for h in range(PEER_HEADS)])
                er = slice(ii * NK + rb * RB, ii * NK + (rb + 1) * RB)
                a = act_ref[er, lanes]
                g2 = a * (1.0 + lax.erf(a * (1.0 / math.sqrt(2.0))))
                w_ref[er, lanes] = g2.astype(BF16) * wsum
    acc_ref[...] += jnp.dot(vt_ref[0], w_ref[...], preferred_element_type=F32)

    @pl.when(c == PEER_NCHUNK - 1)
    def _():
        o_ref[...] = x1_ref[...] + acc_ref[...].T


def _peer(xn, x1, u, vt, rk, e1, cnt, e0):
    T = xn.shape[0]
    H, NK = PEER_HEADS, PEER_NKEYS
    rspec = pl.BlockSpec((H, NK, PEER_TILE), lambda t, c: (0, 0, t))
    return pl.pallas_call(
        _peer_kernel,
        grid=(T // PEER_TILE, PEER_NCHUNK),
        in_specs=[pl.BlockSpec((PEER_TILE, D_MODEL), lambda t, c: (t, 0)),
                  pl.BlockSpec((PEER_TILE, D_MODEL), lambda t, c: (t, 0)),
                  pl.BlockSpec((PEER_ECHUNK, D_MODEL), lambda t, c: (c, 0)),
                  pl.BlockSpec((1, D_MODEL, PEER_ECHUNK), lambda t, c: (c, 0, 0)),
                  rspec, rspec, rspec, rspec],
        out_specs=pl.BlockSpec((PEER_TILE, D_MODEL), lambda t, c: (t, 0)),
        out_shape=jax.ShapeDtypeStruct((T, D_MODEL), F32),
        scratch_shapes=[pltpu.VMEM((D_MODEL, PEER_TILE), F32),
                        pltpu.VMEM((PEER_ECHUNK, PEER_TILE), F32),
                        pltpu.VMEM((PEER_ECHUNK, PEER_TILE), BF16)],
        compiler_params=_params("arbitrary", "arbitrary"),
        name="peer",
    )(xn, x1, u, vt, rk, e1, cnt, e0)


def _layer(x, norm1_w, w_in, w_alpha, b_alpha, out_norm_w, qn_w, kn_w, mix_scale, w_out,
           norm2_w, w_query, subkeys, u_tab, v_tab):
    B, S, D = x.shape
    T = B * S
    x2 = x.reshape(T, D)

    o_gv = 2 * GLA_QK
    o_gate = o_gv + GLA_V
    o_gr = o_gate + GLA_V
    o_mq = o_gr + GLA_RANK
    o_mv = o_mq + 2 * MOBA_W
    w_r = jnp.concatenate([w_in[:, :o_gv], w_in[:, o_gate:o_gr], w_in[:, o_mq:o_mv], w_in[:, o_gr:o_mq],
                           jnp.zeros((D, 128 - GLA_RANK), w_in.dtype)], axis=1).astype(BF16)
    w_v = jnp.concatenate([w_in[:, o_gv:o_gate], w_in[:, o_mv:]], axis=1).astype(BF16)
    wa = jnp.concatenate([w_alpha, jnp.zeros((128 - GLA_RANK, GLA_QK), w_alpha.dtype)], axis=0)

    proj, projv = _inproj(x2, norm1_w.reshape(1, D), w_r, w_v)
    o_gla = _gla(proj, projv, B, S, wa, b_alpha.reshape(1, GLA_QK), out_norm_w.reshape(1, GLA_DV),
                 mix_scale[:GLA_V].reshape(1, GLA_V))

    half = MOBA_HD // 2
    inv = ROPE_THETA ** (-jnp.arange(half, dtype=F32) / half)
    ang = inv[:, None] * jnp.arange(S, dtype=F32)[None, :]
    qw = jnp.broadcast_to(qn_w.reshape(MOBA_HD, 1), (MOBA_HD, MOBA_BLOCK))
    kw = jnp.broadcast_to(kn_w.reshape(MOBA_HD, 1), (MOBA_HD, MOBA_BLOCK))
    qt, kh, vt, bias = _moba_prep(proj, projv, B, S, qw, kw, jnp.cos(ang), jnp.sin(ang))
    o_moba_t = _moba(qt, kh, vt, bias, B, S)

    H, hd = PEER_HEADS, PEER_QDIM // 2
    wq = w_query.reshape(D, H, 2, hd).transpose(0, 2, 1, 3).reshape(D, 2 * H * hd).astype(BF16)

    x1, xn, q = _outproj(x2, o_gla, o_moba_t, mix_scale[GLA_V:].reshape(1, MOBA_W),
                         w_out.astype(BF16), norm2_w.reshape(1, D), wq, B, S)
    rk, e1, cnt, e0 = _route(q, subkeys.astype(BF16))
    vt = v_tab.reshape(PEER_NCHUNK, PEER_ECHUNK, D).transpose(0, 2, 1).astype(BF16)
    out = _peer(xn, x1, u_tab.astype(BF16), vt, rk, e1, cnt, e0)
    return out.reshape(B, S, D)


def kernel(x, norm1_w, w_in, gla_w_alpha, gla_b_alpha, gla_out_norm_w, moba_q_norm_w, moba_k_norm_w,
           mix_scale, w_out, norm2_w, peer_w_query, peer_subkeys, peer_u, peer_v):
    assert norm1_w.shape[0] == 1, "single-layer kernel"
    return _layer(x, norm1_w[0], w_in[0], gla_w_alpha[0], gla_b_alpha[0], gla_out_norm_w[0],
                  moba_q_norm_w[0], moba_k_norm_w[0], mix_scale[0], w_out[0], norm2_w[0],
                  peer_w_query[0], peer_subkeys[0], peer_u[0], peer_v[0])
```

```python
import math

import jax
import jax.numpy as jnp
from jax import lax
from jax.experimental import pallas as pl
from jax.experimental.pallas import tpu as pltpu

D_MODEL = 1024
GLA_HEADS, GLA_DK, GLA_DV, GLA_RANK, GLA_TAU, GLA_CHUNK = 4, 64, 128, 16, 16.0, 64
MOBA_HEADS, MOBA_HD, MOBA_BLOCK, MOBA_TOPK = 8, 64, 256, 3
ROPE_THETA = 10000.0
PEER_HEADS, PEER_NKEYS, PEER_QDIM, PEER_TOPK = 8, 128, 256, 16
PEER_N = PEER_NKEYS * PEER_NKEYS
EPS = 1e-6

GLA_QK = GLA_HEADS * GLA_DK
GLA_V = GLA_HEADS * GLA_DV
MOBA_W = MOBA_HEADS * MOBA_HD
PROJ_W = 2 * GLA_QK + GLA_V + 2 * MOBA_W + 128
GR_COLBLOCK = (PROJ_W - 128) // 128
PROJV_W = GLA_V + MOBA_W

F32 = jnp.float32
BF16 = jnp.bfloat16
HIGHEST = lax.Precision.HIGHEST
NEG = -1e30
LOG2E = math.log2(math.e)

VMEM_LIMIT_BYTES = 56 * 1024 * 1024

IN_TILE = 1024
GLA_TILE = 512
MOBA_VROWS = MOBA_HD + 16
OUT_TILE = 512
ROUTE_TILE = 256
ROUTE_LANES = 128
PEER_TILE = 512
PEER_ECHUNK = 2048
PEER_ACT_SPLITS = (1024,)
PEER_STRIP_ROWS = 32
LANES = 256


def _params(*sem):
    return pltpu.CompilerParams(dimension_semantics=sem, vmem_limit_bytes=VMEM_LIMIT_BYTES)


def _nt_dot(a, b, precision=None):
    return lax.dot_general(a, b, (((1,), (1,)), ((), ())), precision=precision,
                           preferred_element_type=F32)


def _inproj_kernel(x_ref, nw_ref, w_ref, wv_ref, o_ref, ov_ref):
    x = x_ref[...]
    ms = jnp.mean(x * x, axis=-1, keepdims=True)
    xn = (x * lax.rsqrt(ms + EPS) * nw_ref[...]).astype(BF16)
    o_ref[...] = jnp.dot(xn, w_ref[...], preferred_element_type=F32)
    ov_ref[...] = jnp.dot(xn, wv_ref[...], preferred_element_type=F32).astype(BF16)


def _inproj(x2, norm_w, w, wv):
    T = x2.shape[0]
    return pl.pallas_call(
        _inproj_kernel,
        grid=(T // IN_TILE,),
        in_specs=[pl.BlockSpec((IN_TILE, D_MODEL), lambda i: (i, 0)),
                  pl.BlockSpec((1, D_MODEL), lambda i: (0, 0)),
                  pl.BlockSpec((D_MODEL, PROJ_W), lambda i: (0, 0)),
                  pl.BlockSpec((D_MODEL, PROJV_W), lambda i: (0, 0))],
        out_specs=[pl.BlockSpec((IN_TILE, PROJ_W), lambda i: (i, 0)),
                   pl.BlockSpec((IN_TILE, PROJV_W), lambda i: (i, 0))],
        out_shape=[jax.ShapeDtypeStruct((T, PROJ_W), F32),
                   jax.ShapeDtypeStruct((T, PROJV_W), BF16)],
        compiler_params=_params("arbitrary"),
        name="inproj",
    )(x2, norm_w, w, wv)


def _gla_kernel(qk_ref, v_ref, gate_ref, gr_ref, wa_ref, ba_ref, nw_ref, ms_ref, o_ref, st_ref):
    @pl.when(pl.program_id(1) == 0)
    def _():
        st_ref[...] = jnp.zeros_like(st_ref)

    C = GLA_CHUNK
    z = jnp.dot(gr_ref[...], wa_ref[...], precision=HIGHEST, preferred_element_type=F32) + ba_ref[...]
    log_a = (jnp.minimum(z, 0.0) - jnp.log1p(jnp.exp(-jnp.abs(z)))) * (1.0 / GLA_TAU)
    row = lax.broadcasted_iota(jnp.int32, (C, C), 0)
    col = lax.broadcasted_iota(jnp.int32, (C, C), 1)
    causal = col <= row
    tril = causal.astype(F32)
    chunks = range(GLA_TILE // C)
    heads = range(GLA_HEADS)
    ks = [slice(h * GLA_DK, (h + 1) * GLA_DK) for h in heads]
    vs = [slice(h * GLA_DV, (h + 1) * GLA_DV) for h in heads]
    sl = [slice(c * C, (c + 1) * C) for c in chunks]
    q_d, k_in, k_st, decay = [], [], [], []
    for c in chunks:
        b = jnp.dot(tril, log_a[sl[c]], precision=HIGHEST, preferred_element_type=F32)
        b_last = b[C - 1:C, :]
        q = qk_ref[sl[c], 0:GLA_QK]
        k = qk_ref[sl[c], GLA_QK:2 * GLA_QK]
        q_d.append(q * jnp.exp(b) * (GLA_DK ** -0.5))
        k_in.append(k * jnp.exp(-b))
        k_st.append((k * jnp.exp(b_last - b)).astype(BF16))
        decay.append(jnp.exp(b_last))
    v = [[v_ref[sl[c], vs[h]] for h in heads] for c in chunks]
    a = [[jnp.where(causal, _nt_dot(q_d[c][:, ks[h]], k_in[c][:, ks[h]], precision=HIGHEST), 0.0)
          for h in heads] for c in chunks]
    o = [[jnp.dot(a[c][h].astype(BF16), v[c][h], preferred_element_type=F32)
          for h in heads] for c in chunks]
    upd = [[jnp.dot(v[c][h].astype(F32).T.astype(BF16), k_st[c][:, ks[h]], preferred_element_type=F32)
            for h in heads] for c in chunks]
    st = [st_ref[h] for h in heads]
    for c in chunks:
        for h in heads:
            o[c][h] = o[c][h] + _nt_dot(q_d[c][:, ks[h]].astype(BF16), st[h].astype(BF16))
            st[h] = st[h] * decay[c][:, ks[h]] + upd[c][h]
    out = []
    for c in chunks:
        for h in heads:
            y = o[c][h]
            y = y * lax.rsqrt(jnp.mean(y * y, axis=-1, keepdims=True) + EPS) * nw_ref[...]
            g = gate_ref[sl[c], vs[h]]
            out.append((y * (g * jax.nn.sigmoid(g)) * ms_ref[:, vs[h]]).astype(BF16))
    for c in chunks:
        for h in heads:
            o_ref[sl[c], vs[h]] = out[c * GLA_HEADS + h]
    for h in heads:
        st_ref[h] = st[h]


def _gla(proj, projv, B, S, w_alpha, b_alpha, out_norm_w, mix_scale_gla):
    nt = S // GLA_TILE
    tok = lambda b, j: b * nt + j
    return pl.pallas_call(
        _gla_kernel,
        grid=(B, nt),
        in_specs=[pl.BlockSpec((GLA_TILE, 2 * GLA_QK), lambda b, j: (tok(b, j), 0)),
                  pl.BlockSpec((GLA_TILE, GLA_V), lambda b, j: (tok(b, j), 0)),
                  pl.BlockSpec((GLA_TILE, GLA_V), lambda b, j: (tok(b, j), 1)),
                  pl.BlockSpec((GLA_TILE, 128), lambda b, j: (tok(b, j), GR_COLBLOCK)),
                  pl.BlockSpec((128, GLA_QK), lambda b, j: (0, 0)),
                  pl.BlockSpec((1, GLA_QK), lambda b, j: (0, 0)),
                  pl.BlockSpec((1, GLA_DV), lambda b, j: (0, 0)),
                  pl.BlockSpec((1, GLA_V), lambda b, j: (0, 0))],
        out_specs=pl.BlockSpec((GLA_TILE, GLA_V), lambda b, j: (tok(b, j), 0)),
        out_shape=jax.ShapeDtypeStruct((B * S, GLA_V), BF16),
        scratch_shapes=[pltpu.VMEM((GLA_HEADS, GLA_DV, GLA_DK), F32)],
        compiler_params=_params("arbitrary", "arbitrary"),
        name="gla",
    )(proj, projv, proj, proj, w_alpha, b_alpha, out_norm_w, mix_scale_gla)


def _norm_rope_t(xt, w_ref, cos_ref, sin_ref):
    half = MOBA_HD // 2
    outs = []
    for h in range(MOBA_HEADS):
        xh = xt[h * MOBA_HD:(h + 1) * MOBA_HD]
        ms = jnp.mean(xh * xh, axis=0, keepdims=True)
        y = xh * lax.rsqrt(ms + EPS) * w_ref[...]
        y1, y2 = y[:half], y[half:]
        c, s = cos_ref[...], sin_ref[...]
        outs.append(y1 * c - y2 * s)
        outs.append(y2 * c + y1 * s)
    return jnp.concatenate(outs, axis=0)


def _moba_prep_kernel(q_ref, k_ref, v_ref, qw_ref, kw_ref, cos_ref, sin_ref,
                      qt_ref, kh_ref, vt_ref, bias_ref, mt_ref):
    j = pl.program_id(1)
    nb = 16

    @pl.when(j == 0)
    def _():
        mt_ref[...] = jnp.zeros_like(mt_ref)

    qt = _norm_rope_t(q_ref[...].T, qw_ref, cos_ref, sin_ref)
    kt = _norm_rope_t(k_ref[...].T, kw_ref, cos_ref, sin_ref)
    qt_ref[0] = (qt * (MOBA_HD ** -0.5 * LOG2E)).astype(BF16)
    kn = kt.T
    for h in range(MOBA_HEADS):
        kh_ref[0, h] = kn[:, h * MOBA_HD:(h + 1) * MOBA_HD].astype(BF16)
    vtt = v_ref[...].astype(F32).T
    ones_tile = (lax.broadcasted_iota(jnp.int32, (MOBA_VROWS - MOBA_HD, MOBA_BLOCK), 0) == 0).astype(BF16)
    for h in range(MOBA_HEADS):
        vt_ref[0, 0, h * MOBA_VROWS:h * MOBA_VROWS + MOBA_HD] = vtt[h * MOBA_HD:(h + 1) * MOBA_HD].astype(BF16)
        vt_ref[0, 0, h * MOBA_VROWS + MOBA_HD:(h + 1) * MOBA_VROWS] = ones_tile

    bs = jnp.dot(mt_ref[...], qt, precision=HIGHEST, preferred_element_type=F32)
    n_idx = lax.broadcasted_iota(jnp.int32, (nb, MOBA_BLOCK), 0)
    for h in range(MOBA_HEADS):
        sc = jnp.where(n_idx < j, bs[h * nb:(h + 1) * nb], -jnp.inf)
        sel = jnp.zeros((nb, MOBA_BLOCK), jnp.bool_)
        for r in range(MOBA_TOPK):
            m = jnp.max(sc, axis=0, keepdims=True)
            first = jnp.min(jnp.where(sc == m, n_idx, nb), axis=0, keepdims=True)
            hit = n_idx == first
            sel = jnp.logical_or(sel, jnp.logical_and(hit, r < j))
            sc = jnp.where(hit, -jnp.inf, sc)
        bias_ref[0, 0, h] = jnp.where(sel, 0.0, NEG)

    kbar = jnp.mean(kn, axis=0, keepdims=True)
    r_idx = lax.broadcasted_iota(jnp.int32, (MOBA_HEADS * nb, MOBA_W), 0)
    c_idx = lax.broadcasted_iota(jnp.int32, (MOBA_HEADS * nb, MOBA_W), 1)
    mine = jnp.logical_and(r_idx % nb == j, r_idx // nb == c_idx // MOBA_HD)
    mt_ref[...] = jnp.where(mine, kbar, mt_ref[...])


def _moba_prep(proj, projv, B, S, qw, kw, cos_t, sin_t):
    nb = S // MOBA_BLOCK
    assert nb <= 16
    tok = lambda b, j: b * nb + j
    half = MOBA_HD // 2
    return pl.pallas_call(
        _moba_prep_kernel,
        grid=(B, nb),
        in_specs=[pl.BlockSpec((MOBA_BLOCK, MOBA_W), lambda b, j: (tok(b, j), 2)),
                  pl.BlockSpec((MOBA_BLOCK, MOBA_W), lambda b, j: (tok(b, j), 3)),
                  pl.BlockSpec((MOBA_BLOCK, MOBA_W), lambda b, j: (tok(b, j), 1)),
                  pl.BlockSpec((MOBA_HD, MOBA_BLOCK), lambda b, j: (0, 0)),
                  pl.BlockSpec((MOBA_HD, MOBA_BLOCK), lambda b, j: (0, 0)),
                  pl.BlockSpec((half, MOBA_BLOCK), lambda b, j: (0, j)),
                  pl.BlockSpec((half, MOBA_BLOCK), lambda b, j: (0, j))],
        out_specs=[pl.BlockSpec((1, MOBA_W, MOBA_BLOCK), lambda b, j: (b, 0, j)),
                   pl.BlockSpec((1, MOBA_HEADS, MOBA_BLOCK, MOBA_HD), lambda b, j: (b, 0, j, 0)),
                   pl.BlockSpec((1, 1, MOBA_HEADS * MOBA_VROWS, MOBA_BLOCK), lambda b, j: (b, j, 0, 0)),
                   pl.BlockSpec((1, 1, MOBA_HEADS, 16, MOBA_BLOCK), lambda b, j: (b, j, 0, 0, 0))],
        out_shape=[jax.ShapeDtypeStruct((B, MOBA_W, S), BF16),
                   jax.ShapeDtypeStruct((B, MOBA_HEADS, S, MOBA_HD), BF16),
                   jax.ShapeDtypeStruct((B, nb, MOBA_HEADS * MOBA_VROWS, MOBA_BLOCK), BF16),
                   jax.ShapeDtypeStruct((B, nb, MOBA_HEADS, 16, MOBA_BLOCK), F32)],
        scratch_shapes=[pltpu.VMEM((MOBA_HEADS * 16, MOBA_W), F32)],
        compiler_params=_params("arbitrary", "arbitrary"),
        name="moba_prep",
    )(proj, proj, projv, qw, kw, cos_t, sin_t)


def _moba_kernel(qt_ref, k_ref, vt_ref, bias_ref, o_ref, acc_ref, m_ref, s0_ref, s1_ref):
    j = pl.program_id(1)
    hd = MOBA_HD
    kpos = lax.broadcasted_iota(jnp.int32, (MOBA_BLOCK, MOBA_BLOCK), 0)
    qpos = lax.broadcasted_iota(jnp.int32, (MOBA_BLOCK, MOBA_BLOCK), 1)
    causal = kpos <= qpos
    own = pl.ds(pl.multiple_of(j * MOBA_BLOCK, MOBA_BLOCK), MOBA_BLOCK)
    heads = range(MOBA_HEADS)
    rows = [slice(h * hd, (h + 1) * hd) for h in heads]
    vrows = [slice(h * MOBA_VROWS, (h + 1) * MOBA_VROWS) for h in heads]
    s = [jnp.where(causal, jnp.dot(k_ref[0, h, own, :], qt_ref[0, rows[h], :], preferred_element_type=F32),
                   -jnp.inf) for h in heads]
    m = [jnp.max(s[h], axis=0, keepdims=True) for h in heads]
    p = [jnp.exp2(s[h] - m[h]) for h in heads]
    pv = [jnp.dot(vt_ref[0, j, vrows[h], :], p[h].astype(BF16), preferred_element_type=F32)
          for h in heads]
    for h in heads:
        m_ref[h] = m[h]
        acc_ref[h] = pv[h]

    nb = k_ref.shape[2] // MOBA_BLOCK

    def scores(n, s_ref):
        nc = jnp.minimum(n, nb - 1)
        blk = pl.ds(pl.multiple_of(nc * MOBA_BLOCK, MOBA_BLOCK), MOBA_BLOCK)
        for h in heads:
            s_ref[h] = (jnp.dot(k_ref[0, h, blk, :], qt_ref[0, rows[h], :], preferred_element_type=F32)
                        + bias_ref[0, 0, h, pl.ds(nc, 1), :])

    def consume(n, s_ref):
        s = {h: s_ref[h] for h in heads}
        m_old = {h: m_ref[h] for h in heads}
        m_new = {h: jnp.maximum(m_old[h], jnp.max(s[h], axis=0, keepdims=True)) for h in heads}
        alpha = {h: jnp.exp2(m_old[h] - m_new[h]) for h in heads}
        p = {h: jnp.exp2(s[h] - m_new[h]) for h in heads}
        pv = {h: jnp.dot(vt_ref[0, n, vrows[h], :], p[h].astype(BF16), preferred_element_type=F32)
              for h in heads}
        acc_new = {h: acc_ref[h] * alpha[h] + pv[h] for h in heads}
        for h in heads:
            m_ref[h] = m_new[h]
            acc_ref[h] = acc_new[h]

    scores(0, s0_ref)

    def pair(nn, carry):
        n = 2 * nn
        scores(n + 1, s1_ref)
        consume(n, s0_ref)
        scores(n + 2, s0_ref)
        consume(n + 1, s1_ref)
        return carry

    lax.fori_loop(0, j // 2, pair, 0)

    @pl.when(j % 2 == 1)
    def _():
        consume(j - 1, s0_ref)

    for h in range(MOBA_HEADS):
        o_ref[0, h * hd:(h + 1) * hd, :] = (acc_ref[h, 0:hd, :] / acc_ref[h, hd:hd + 1, :]).astype(BF16)


def _moba(qt, kh, vt, bias, B, S):
    nb = S // MOBA_BLOCK
    return pl.pallas_call(
        _moba_kernel,
        grid=(B, nb),
        in_specs=[pl.BlockSpec((1, MOBA_W, MOBA_BLOCK), lambda b, j: (b, 0, j)),
                  pl.BlockSpec((1, MOBA_HEADS, S, MOBA_HD), lambda b, j: (b, 0, 0, 0)),
                  pl.BlockSpec((1, nb, MOBA_HEADS * MOBA_VROWS, MOBA_BLOCK), lambda b, j: (b, 0, 0, 0)),
                  pl.BlockSpec((1, 1, MOBA_HEADS, 16, MOBA_BLOCK), lambda b, j: (b, j, 0, 0, 0))],
        out_specs=pl.BlockSpec((1, MOBA_W, MOBA_BLOCK), lambda b, j: (b, 0, j)),
        out_shape=jax.ShapeDtypeStruct((B, MOBA_W, S), BF16),
        scratch_shapes=[pltpu.VMEM((MOBA_HEADS, MOBA_VROWS, MOBA_BLOCK), F32),
                        pltpu.VMEM((MOBA_HEADS, 1, MOBA_BLOCK), F32),
                        pltpu.VMEM((MOBA_HEADS, MOBA_BLOCK, MOBA_BLOCK), F32),
                        pltpu.VMEM((MOBA_HEADS, MOBA_BLOCK, MOBA_BLOCK), F32)],
        compiler_params=_params("arbitrary", "arbitrary"),
        name="moba",
    )(qt, kh, vt, bias)


def _outproj_kernel(x_ref, og_ref, omt_ref, msm_ref, wo_ref, n2_ref, wq_ref, x1_ref, xn_ref, q_ref):
    om = (omt_ref[0].astype(F32).T * msm_ref[...]).astype(BF16)
    x1 = (x_ref[...]
          + jnp.dot(og_ref[...], wo_ref[0:GLA_V, :], preferred_element_type=F32)
          + jnp.dot(om, wo_ref[GLA_V:, :], preferred_element_type=F32))
    x1_ref[...] = x1
    ms = jnp.mean(x1 * x1, axis=-1, keepdims=True)
    xn = (x1 * lax.rsqrt(ms + EPS) * n2_ref[...]).astype(BF16)
    xn_ref[...] = xn
    q_ref[...] = jnp.dot(xn, wq_ref[...], preferred_element_type=F32).astype(BF16)


def _outproj(x2, o_gla, o_moba_t, ms_moba, w_out, norm2_w, wq, B, S):
    T = B * S
    nt = S // OUT_TILE
    QW = PEER_HEADS * PEER_QDIM
    return pl.pallas_call(
        _outproj_kernel,
        grid=(T // OUT_TILE,),
        in_specs=[pl.BlockSpec((OUT_TILE, D_MODEL), lambda i: (i, 0)),
                  pl.BlockSpec((OUT_TILE, GLA_V), lambda i: (i, 0)),
                  pl.BlockSpec((1, MOBA_W, OUT_TILE), lambda i: (i // nt, 0, i % nt)),
                  pl.BlockSpec((1, MOBA_W), lambda i: (0, 0)),
                  pl.BlockSpec((GLA_V + MOBA_W, D_MODEL), lambda i: (0, 0)),
                  pl.BlockSpec((1, D_MODEL), lambda i: (0, 0)),
                  pl.BlockSpec((D_MODEL, QW), lambda i: (0, 0))],
        out_specs=[pl.BlockSpec((OUT_TILE, D_MODEL), lambda i: (i, 0)),
                   pl.BlockSpec((OUT_TILE, D_MODEL), lambda i: (i, 0)),
                   pl.BlockSpec((OUT_TILE, QW), lambda i: (i, 0))],
        out_shape=[jax.ShapeDtypeStruct((T, D_MODEL), F32),
                   jax.ShapeDtypeStruct((T, D_MODEL), BF16),
                   jax.ShapeDtypeStruct((T, QW), BF16)],
        compiler_params=_params("arbitrary"),
        name="outproj",
    )(x2, o_gla, o_moba_t, ms_moba, w_out, norm2_w, wq)


_PAIRS = [(r, q) for r in range(PEER_TOPK + 1) for q in range(PEER_TOPK + 1)
          if (r + 1) * (q + 1) <= PEER_TOPK + 1]


def _tree(op, xs):
    xs = list(xs)
    while len(xs) > 1:
        xs = [op(xs[i], xs[i + 1]) if i + 1 < len(xs) else xs[i] for i in range(0, len(xs), 2)]
    return xs[0]


def _route_kernel(q_ref, sk_ref, rk_ref, e1_ref, cnt_ref, e0_ref, sc_ref, tmp_ref):
    H, NK = PEER_HEADS, PEER_NKEYS
    hq = H * PEER_QDIM // 2
    G = 8
    NTOP = PEER_TOPK + 1
    hd = PEER_QDIM // 2
    for p in range(2):
        for h in range(H):
            sc = _nt_dot(sk_ref[h, p], q_ref[:, (p * H + h) * hd:(p * H + h + 1) * hd])
            for lt in range(ROUTE_TILE // ROUTE_LANES):
                sc_ref[p, lt, pl.ds(h, NK, stride=H), :] = sc[:, lt * ROUTE_LANES:(lt + 1) * ROUTE_LANES]
    neg = jnp.full((H, ROUTE_LANES), -jnp.inf, F32)

    def insert(top, v):
        out = []
        for t in top:
            out.append(jnp.maximum(t, v))
            v = jnp.minimum(t, v)
        return out

    for lt in range(ROUTE_TILE // ROUTE_LANES):
        lanes = slice(lt * ROUTE_LANES, (lt + 1) * ROUTE_LANES)

        def largest(p, lt=lt):
            def body(g, top):
                v = sc_ref[p, lt, pl.ds(pl.multiple_of(g * (G * H), G * H), G * H), :]
                top = list(top)
                for k in range(G):
                    top = insert(top, v[k * H:(k + 1) * H])
                return tuple(top)
            return lax.fori_loop(0, NK // G, body, (neg,) * NTOP)

        tops = (largest(0), largest(1))
        best = [neg] * NTOP
        for r, q in _PAIRS:
            best = insert(best, tops[0][r] + tops[1][q])
        tau = 0.5 * (best[PEER_TOPK - 1] + best[PEER_TOPK])
        top = tops[0][0] + tops[1][0]
        zinv = 0.5 / _tree(jnp.add, [jnp.where(tops[0][r] + tops[1][q] > tau,
                                               jnp.exp(tops[0][r] + tops[1][q] - top), 0.0)
                                     for r, q in _PAIRS])
        thr = [tau - b for b in tops[1][:PEER_TOPK]]
        top1 = tops[1][:PEER_TOPK]

        def half0(g, carry, lt=lt, tops=tops, thr=thr, zinv=zinv):
            base = pl.multiple_of(g * (G * H), G * H)
            v0 = sc_ref[0, lt, pl.ds(base, G * H), :]
            for k in range(G):
                a = v0[k * H:(k + 1) * H]
                rows = pl.ds(pl.multiple_of(base + k * H, H), H)
                tmp_ref[0, rows, :] = _tree(jnp.add, [jnp.where(a > t, 1.0, 0.0) for t in thr])
                tmp_ref[1, rows, :] = jnp.exp(a - tops[0][0]) * zinv
            return carry

        def half1(g, carry, lt=lt, tops=tops, top1=top1):
            base = pl.multiple_of(g * (G * H), G * H)
            v1 = sc_ref[1, lt, pl.ds(base, G * H), :]
            for k in range(G):
                b = v1[k * H:(k + 1) * H]
                rows = pl.ds(pl.multiple_of(base + k * H, H), H)
                tmp_ref[2, rows, :] = 1.0 + _tree(jnp.add, [jnp.where(b < t, 1.0, 0.0) for t in top1])
                tmp_ref[3, rows, :] = jnp.exp(b - tops[1][0])
            return carry

        lax.fori_loop(0, NK // G, half0, 0)
        lax.fori_loop(0, NK // G, half1, 0)
        for h in range(H):
            cnt_ref[h, :, lanes] = tmp_ref[0, pl.ds(h, NK, stride=H), :]
            e0_ref[h, :, lanes] = tmp_ref[1, pl.ds(h, NK, stride=H), :]
            rk_ref[h, :, lanes] = tmp_ref[2, pl.ds(h, NK, stride=H), :].astype(BF16)
            e1_ref[h, :, lanes] = tmp_ref[3, pl.ds(h, NK, stride=H), :].astype(BF16)


def _route(q, subkeys):
    T = q.shape[0]
    H, NK = PEER_HEADS, PEER_NKEYS
    hq = H * PEER_QDIM // 2
    spec = pl.BlockSpec((H, NK, ROUTE_TILE), lambda i: (0, 0, i))
    return pl.pallas_call(
        _route_kernel,
        grid=(T // ROUTE_TILE,),
        in_specs=[pl.BlockSpec((ROUTE_TILE, 2 * hq), lambda i: (i, 0)),
                  pl.BlockSpec((H, 2, NK, PEER_QDIM // 2), lambda i: (0, 0, 0, 0))],
        out_specs=[spec, spec, spec, spec],
        out_shape=[jax.ShapeDtypeStruct((H, NK, T), BF16), jax.ShapeDtypeStruct((H, NK, T), BF16),
                   jax.ShapeDtypeStruct((H, NK, T), F32), jax.ShapeDtypeStruct((H, NK, T), F32)],
        scratch_shapes=[pltpu.VMEM((2, ROUTE_TILE // ROUTE_LANES, NK * H, ROUTE_LANES), F32),
                        pltpu.VMEM((4, NK * H, ROUTE_LANES), F32)],
        compiler_params=_params("arbitrary"),
        name="peer_route",
    )(q, subkeys)


PEER_NCHUNK = PEER_N // PEER_ECHUNK


def _peer_kernel(xn_ref, x1_ref, u_ref, vt_ref, rk_ref, e1_ref, cnt_ref, e0_ref, o_ref,
                 acc_ref, act_ref, w_ref):
    c = pl.program_id(1)
    NK = PEER_NKEYS
    nsub = PEER_ECHUNK // NK
    RB = PEER_STRIP_ROWS

    @pl.when(c == 0)
    def _():
        acc_ref[...] = jnp.zeros_like(acc_ref)

    bounds = (0,) + PEER_ACT_SPLITS + (PEER_ECHUNK,)
    for ii in range(nsub):
        if ii * NK in bounds[:-1]:
            hs = slice(ii * NK, bounds[bounds.index(ii * NK) + 1])
            act_ref[hs, :] = _nt_dot(u_ref[hs, :], xn_ref[...])
        i = c * nsub + ii
        cnt_rows = [cnt_ref[h, pl.ds(i, 1), :] for h in range(PEER_HEADS)]
        e0_rows = [e0_ref[h, pl.ds(i, 1), :] for h in range(PEER_HEADS)]
        for lc in range(PEER_TILE // LANES):
            lanes = slice(lc * LANES, (lc + 1) * LANES)
            cnts = [jnp.broadcast_to(r[:, lanes], (RB, LANES)).astype(BF16) for r in cnt_rows]
            e0s = [jnp.broadcast_to(r[:, lanes], (RB, LANES)).astype(BF16) for r in e0_rows]
            for rb in range(NK // RB):
                rows = slice(rb * RB, (rb + 1) * RB)
                wsum = _tree(jnp.add, [
                    e0s[h] * jnp.where(rk_ref[h, rows, lanes] <= cnts[h], e1_ref[h, rows, lanes], 0.0)
                    for h in range(PEER_HEADS)])
                er = slice(ii * NK + rb * RB, ii * NK + (rb + 1) * RB)
                a = act_ref[er, lanes]
                g2 = a * (1.0 + lax.erf(a * (1.0 / math.sqrt(2.0))))
                w_ref[er, lanes] = g2.astype(BF16) * wsum
    acc_ref[...] += jnp.dot(vt_ref[0], w_ref[...], preferred_element_type=F32)

    @pl.when(c == PEER_NCHUNK - 1)
    def _():
        o_ref[...] = x1_ref[...] + acc_ref[...].T


def _peer(xn, x1, u, vt, rk, e1, cnt, e0):
    T = xn.shape[0]
    H, NK = PEER_HEADS, PEER_NKEYS
    rspec = pl.BlockSpec((H, NK, PEER_TILE), lambda t, c: (0, 0, t))
    return pl.pallas_call(
        _peer_kernel,
        grid=(T // PEER_TILE, PEER_NCHUNK),
        in_specs=[pl.BlockSpec((PEER_TILE, D_MODEL), lambda t, c: (t, 0)),
                  pl.BlockSpec((PEER_TILE, D_MODEL), lambda t, c: (t, 0)),
                  pl.BlockSpec((PEER_ECHUNK, D_MODEL), lambda t, c: (c, 0)),
                  pl.BlockSpec((1, D_MODEL, PEER_ECHUNK), lambda t, c: (c, 0, 0)),
                  rspec, rspec, rspec, rspec],
        out_specs=pl.BlockSpec((PEER_TILE, D_MODEL), lambda t, c: (t, 0)),
        out_shape=jax.ShapeDtypeStruct((T, D_MODEL), F32),
        scratch_shapes=[pltpu.VMEM((D_MODEL, PEER_TILE), F32),
                        pltpu.VMEM((PEER_ECHUNK, PEER_TILE), F32),
                        pltpu.VMEM((PEER_ECHUNK, PEER_TILE), BF16)],
        compiler_params=_params("arbitrary", "arbitrary"),
        name="peer",
    )(xn, x1, u, vt, rk, e1, cnt, e0)


def _layer(x, norm1_w, w_in, w_alpha, b_alpha, out_norm_w, qn_w, kn_w, mix_scale, w_out,
           norm2_w, w_query, subkeys, u_tab, v_tab):
    B, S, D = x.shape
    T = B * S
    x2 = x.reshape(T, D)

    o_gv = 2 * GLA_QK
    o_gate = o_gv + GLA_V
    o_gr = o_gate + GLA_V
    o_mq = o_gr + GLA_RANK
    o_mv = o_mq + 2 * MOBA_W
    w_r = jnp.concatenate([w_in[:, :o_gv], w_in[:, o_gate:o_gr], w_in[:, o_mq:o_mv], w_in[:, o_gr:o_mq],
                           jnp.zeros((D, 128 - GLA_RANK), w_in.dtype)], axis=1).astype(BF16)
    w_v = jnp.concatenate([w_in[:, o_gv:o_gate], w_in[:, o_mv:]], axis=1).astype(BF16)
    wa = jnp.concatenate([w_alpha, jnp.zeros((128 - GLA_RANK, GLA_QK), w_alpha.dtype)], axis=0)

    proj, projv = _inproj(x2, norm1_w.reshape(1, D), w_r, w_v)
    o_gla = _gla(proj, projv, B, S, wa, b_alpha.reshape(1, GLA_QK), out_norm_w.reshape(1, GLA_DV),
                 mix_scale[:GLA_V].reshape(1, GLA_V))

    half = MOBA_HD // 2
    inv = ROPE_THETA ** (-jnp.arange(half, dtype=F32) / half)
    ang = inv[:, None] * jnp.arange(S, dtype=F32)[None, :]
    qw = jnp.broadcast_to(qn_w.reshape(MOBA_HD, 1), (MOBA_HD, MOBA_BLOCK))
    kw = jnp.broadcast_to(kn_w.reshape(MOBA_HD, 1), (MOBA_HD, MOBA_BLOCK))
    qt, kh, vt, bias = _moba_prep(proj, projv, B, S, qw, kw, jnp.cos(ang), jnp.sin(ang))
    o_moba_t = _moba(qt, kh, vt, bias, B, S)

    H, hd = PEER_HEADS, PEER_QDIM // 2
    wq = w_query.reshape(D, H, 2, hd).transpose(0, 2, 1, 3).reshape(D, 2 * H * hd).astype(BF16)

    x1, xn, q = _outproj(x2, o_gla, o_moba_t, mix_scale[GLA_V:].reshape(1, MOBA_W),
                         w_out.astype(BF16), norm2_w.reshape(1, D), wq, B, S)
    rk, e1, cnt, e0 = _route(q, subkeys.astype(BF16))
    vt = v_tab.reshape(PEER_NCHUNK, PEER_ECHUNK, D).transpose(0, 2, 1).astype(BF16)
    out = _peer(xn, x1, u_tab.astype(BF16), vt, rk, e1, cnt, e0)
    return out.reshape(B, S, D)


def kernel(x, norm1_w, w_in, gla_w_alpha, gla_b_alpha, gla_out_norm_w, moba_q_norm_w, moba_k_norm_w,
           mix_scale, w_out, norm2_w, peer_w_query, peer_subkeys, peer_u, peer_v):
    assert norm1_w.shape[0] == 1, "single-layer kernel"
    return _layer(x, norm1_w[0], w_in[0], gla_w_alpha[0], gla_b_alpha[0], gla_out_norm_w[0],
                  moba_q_norm_w[0], moba_k_norm_w[0], mix_scale[0], w_out[0], norm2_w[0],
                  peer_w_query[0], peer_subkeys[0], peer_u[0], peer_v[0])
```

```python
import math

import jax
import jax.numpy as jnp
from jax import lax
from jax.experimental import pallas as pl
from jax.experimental.pallas import tpu as pltpu

D_MODEL = 1024
GLA_HEADS, GLA_DK, GLA_DV, GLA_RANK, GLA_TAU, GLA_CHUNK = 4, 64, 128, 16, 16.0, 64
MOBA_HEADS, MOBA_HD, MOBA_BLOCK, MOBA_TOPK = 8, 64, 256, 3
ROPE_THETA = 10000.0
PEER_HEADS, PEER_NKEYS, PEER_QDIM, PEER_TOPK = 8, 128, 256, 16
PEER_N = PEER_NKEYS * PEER_NKEYS
EPS = 1e-6

GLA_QK = GLA_HEADS * GLA_DK
GLA_V = GLA_HEADS * GLA_DV
MOBA_W = MOBA_HEADS * MOBA_HD
PROJ_W = 2 * GLA_QK + GLA_V + 2 * MOBA_W + 128
GR_COLBLOCK = (PROJ_W - 128) // 128
PROJV_W = GLA_V + MOBA_W

F32 = jnp.float32
BF16 = jnp.bfloat16
HIGHEST = lax.Precision.HIGHEST
NEG = -1e30
LOG2E = math.log2(math.e)

VMEM_LIMIT_BYTES = 56 * 1024 * 1024

IN_TILE = 1024
GLA_TILE = 512
MOBA_VROWS = MOBA_HD + 16
OUT_TILE = 512
ROUTE_TILE = 256
ROUTE_LANES = 128
PEER_TILE = 512
PEER_ECHUNK = 2048
PEER_ACT_SPLITS = (1024,)
PEER_STRIP_ROWS = 32
LANES = 256


def _params(*sem):
    return pltpu.CompilerParams(dimension_semantics=sem, vmem_limit_bytes=VMEM_LIMIT_BYTES)


def _nt_dot(a, b, precision=None):
    return lax.dot_general(a, b, (((1,), (1,)), ((), ())), precision=precision,
                           preferred_element_type=F32)


def _inproj_kernel(x_ref, nw_ref, w_ref, wv_ref, o_ref, ov_ref):
    x = x_ref[...]
    ms = jnp.mean(x * x, axis=-1, keepdims=True)
    xn = (x * lax.rsqrt(ms + EPS) * nw_ref[...]).astype(BF16)
    o_ref[...] = jnp.dot(xn, w_ref[...], preferred_element_type=F32)
    ov_ref[...] = jnp.dot(xn, wv_ref[...], preferred_element_type=F32).astype(BF16)


def _inproj(x2, norm_w, w, wv):
    T = x2.shape[0]
    return pl.pallas_call(
        _inproj_kernel,
        grid=(T // IN_TILE,),
        in_specs=[pl.BlockSpec((IN_TILE, D_MODEL), lambda i: (i, 0)),
                  pl.BlockSpec((1, D_MODEL), lambda i: (0, 0)),
                  pl.BlockSpec((D_MODEL, PROJ_W), lambda i: (0, 0)),
                  pl.BlockSpec((D_MODEL, PROJV_W), lambda i: (0, 0))],
        out_specs=[pl.BlockSpec((IN_TILE, PROJ_W), lambda i: (i, 0)),
                   pl.BlockSpec((IN_TILE, PROJV_W), lambda i: (i, 0))],
        out_shape=[jax.ShapeDtypeStruct((T, PROJ_W), F32),
                   jax.ShapeDtypeStruct((T, PROJV_W), BF16)],
        compiler_params=_params("arbitrary"),
        name="inproj",
    )(x2, norm_w, w, wv)


def _gla_kernel(qk_ref, v_ref, gate_ref, gr_ref, wa_ref, ba_ref, nw_ref, ms_ref, o_ref, st_ref):
    @pl.when(pl.program_id(1) == 0)
    def _():
        st_ref[...] = jnp.zeros_like(st_ref)

    C = GLA_CHUNK
    row = lax.broadcasted_iota(jnp.int32, (C, C), 0)
    col = lax.broadcasted_iota(jnp.int32, (C, C), 1)
    causal = col <= row
    tril = causal.astype(F32)
    chunks = range(GLA_TILE // C)
    heads = range(GLA_HEADS)
    ks = [slice(h * GLA_DK, (h + 1) * GLA_DK) for h in heads]
    vs = [slice(h * GLA_DV, (h + 1) * GLA_DV) for h in heads]
    sl = [slice(c * C, (c + 1) * C) for c in chunks]
    q_d, k_in, k_st, decay = [], [], [], []
    for c in chunks:
        z = jnp.dot(gr_ref[sl[c], :], wa_ref[...], precision=HIGHEST, preferred_element_type=F32) + ba_ref[...]
        log_a = (jnp.minimum(z, 0.0) - jnp.log1p(jnp.exp(-jnp.abs(z)))) * (1.0 / GLA_TAU)
        b = jnp.dot(tril, log_a, precision=HIGHEST, preferred_element_type=F32)
        b_last = b[C - 1:C, :]
        q = qk_ref[sl[c], 0:GLA_QK]
        k = qk_ref[sl[c], GLA_QK:2 * GLA_QK]
        q_d.append(q * jnp.exp(b) * (GLA_DK ** -0.5))
        k_in.append(k * jnp.exp(-b))
        k_st.append((k * jnp.exp(b_last - b)).astype(BF16))
        decay.append(jnp.exp(b_last))
    v = [[v_ref[sl[c], vs[h]] for h in heads] for c in chunks]
    a = [[jnp.where(causal, _nt_dot(q_d[c][:, ks[h]], k_in[c][:, ks[h]], precision=HIGHEST), 0.0)
          for h in heads] for c in chunks]
    o = [[jnp.dot(a[c][h].astype(BF16), v[c][h], preferred_element_type=F32)
          for h in heads] for c in chunks]
    upd = [[jnp.dot(v[c][h].astype(F32).T.astype(BF16), k_st[c][:, ks[h]], preferred_element_type=F32)
            for h in heads] for c in chunks]
    st = [st_ref[h] for h in heads]
    for c in chunks:
        for h in heads:
            o[c][h] = o[c][h] + _nt_dot(q_d[c][:, ks[h]].astype(BF16), st[h].astype(BF16))
            st[h] = st[h] * decay[c][:, ks[h]] + upd[c][h]
    out = []
    for c in chunks:
        for h in heads:
            y = o[c][h]
            y = y * lax.rsqrt(jnp.mean(y * y, axis=-1, keepdims=True) + EPS) * nw_ref[...]
            g = gate_ref[sl[c], vs[h]]
            out.append((y * (g * jax.nn.sigmoid(g)) * ms_ref[:, vs[h]]).astype(BF16))
    for c in chunks:
        for h in heads:
            o_ref[sl[c], vs[h]] = out[c * GLA_HEADS + h]
    for h in heads:
        st_ref[h] = st[h]


def _gla(proj, projv, B, S, w_alpha, b_alpha, out_norm_w, mix_scale_gla):
    nt = S // GLA_TILE
    tok = lambda b, j: b * nt + j
    return pl.pallas_call(
        _gla_kernel,
        grid=(B, nt),
        in_specs=[pl.BlockSpec((GLA_TILE, 2 * GLA_QK), lambda b, j: (tok(b, j), 0)),
                  pl.BlockSpec((GLA_TILE, GLA_V), lambda b, j: (tok(b, j), 0)),
                  pl.BlockSpec((GLA_TILE, GLA_V), lambda b, j: (tok(b, j), 1)),
                  pl.BlockSpec((GLA_TILE, 128), lambda b, j: (tok(b, j), GR_COLBLOCK)),
                  pl.BlockSpec((128, GLA_QK), lambda b, j: (0, 0)),
                  pl.BlockSpec((1, GLA_QK), lambda b, j: (0, 0)),
                  pl.BlockSpec((1, GLA_DV), lambda b, j: (0, 0)),
                  pl.BlockSpec((1, GLA_V), lambda b, j: (0, 0))],
        out_specs=pl.BlockSpec((GLA_TILE, GLA_V), lambda b, j: (tok(b, j), 0)),
        out_shape=jax.ShapeDtypeStruct((B * S, GLA_V), BF16),
        scratch_shapes=[pltpu.VMEM((GLA_HEADS, GLA_DV, GLA_DK), F32)],
        compiler_params=_params("arbitrary", "arbitrary"),
        name="gla",
    )(proj, projv, proj, proj, w_alpha, b_alpha, out_norm_w, mix_scale_gla)


def _norm_rope_t(xt, w_ref, cos_ref, sin_ref):
    half = MOBA_HD // 2
    outs = []
    for h in range(MOBA_HEADS):
        xh = xt[h * MOBA_HD:(h + 1) * MOBA_HD]
        ms = jnp.mean(xh * xh, axis=0, keepdims=True)
        y = xh * lax.rsqrt(ms + EPS) * w_ref[...]
        y1, y2 = y[:half], y[half:]
        c, s = cos_ref[...], sin_ref[...]
        outs.append(y1 * c - y2 * s)
        outs.append(y2 * c + y1 * s)
    return jnp.concatenate(outs, axis=0)


def _moba_prep_kernel(q_ref, k_ref, v_ref, qw_ref, kw_ref, cos_ref, sin_ref,
                      qt_ref, kh_ref, vt_ref, bias_ref, mt_ref):
    j = pl.program_id(1)
    nb = 16

    @pl.when(j == 0)
    def _():
        mt_ref[...] = jnp.zeros_like(mt_ref)

    qt = _norm_rope_t(q_ref[...].T, qw_ref, cos_ref, sin_ref)
    kt = _norm_rope_t(k_ref[...].T, kw_ref, cos_ref, sin_ref)
    qt_ref[0] = (qt * (MOBA_HD ** -0.5 * LOG2E)).astype(BF16)
    kn = kt.T
    for h in range(MOBA_HEADS):
        kh_ref[0, h] = kn[:, h * MOBA_HD:(h + 1) * MOBA_HD].astype(BF16)
    vtt = v_ref[...].astype(F32).T
    ones_tile = (lax.broadcasted_iota(jnp.int32, (MOBA_VROWS - MOBA_HD, MOBA_BLOCK), 0) == 0).astype(BF16)
    for h in range(MOBA_HEADS):
        vt_ref[0, 0, h * MOBA_VROWS:h * MOBA_VROWS + MOBA_HD] = vtt[h * MOBA_HD:(h + 1) * MOBA_HD].astype(BF16)
        vt_ref[0, 0, h * MOBA_VROWS + MOBA_HD:(h + 1) * MOBA_VROWS] = ones_tile

    bs = jnp.dot(mt_ref[...], qt, precision=HIGHEST, preferred_element_type=F32)
    n_idx = lax.broadcasted_iota(jnp.int32, (nb, MOBA_BLOCK), 0)
    for h in range(MOBA_HEADS):
        sc = jnp.where(n_idx < j, bs[h * nb:(h + 1) * nb], -jnp.inf)
        sel = jnp.zeros((nb, MOBA_BLOCK), jnp.bool_)
        for r in range(MOBA_TOPK):
            m = jnp.max(sc, axis=0, keepdims=True)
            first = jnp.min(jnp.where(sc == m, n_idx, nb), axis=0, keepdims=True)
            hit = n_idx == first
            sel = jnp.logical_or(sel, jnp.logical_and(hit, r < j))
            sc = jnp.where(hit, -jnp.inf, sc)
        bias_ref[0, 0, h] = jnp.where(sel, 0.0, NEG)

    kbar = jnp.mean(kn, axis=0, keepdims=True)
    r_idx = lax.broadcasted_iota(jnp.int32, (MOBA_HEADS * nb, MOBA_W), 0)
    c_idx = lax.broadcasted_iota(jnp.int32, (MOBA_HEADS * nb, MOBA_W), 1)
    mine = jnp.logical_and(r_idx % nb == j, r_idx // nb == c_idx // MOBA_HD)
    mt_ref[...] = jnp.where(mine, kbar, mt_ref[...])


def _moba_prep(proj, projv, B, S, qw, kw, cos_t, sin_t):
    nb = S // MOBA_BLOCK
    assert nb <= 16
    tok = lambda b, j: b * nb + j
    half = MOBA_HD // 2
    return pl.pallas_call(
        _moba_prep_kernel,
        grid=(B, nb),
        in_specs=[pl.BlockSpec((MOBA_BLOCK, MOBA_W), lambda b, j: (tok(b, j), 2)),
                  pl.BlockSpec((MOBA_BLOCK, MOBA_W), lambda b, j: (tok(b, j), 3)),
                  pl.BlockSpec((MOBA_BLOCK, MOBA_W), lambda b, j: (tok(b, j), 1)),
                  pl.BlockSpec((MOBA_HD, MOBA_BLOCK), lambda b, j: (0, 0)),
                  pl.BlockSpec((MOBA_HD, MOBA_BLOCK), lambda b, j: (0, 0)),
                  pl.BlockSpec((half, MOBA_BLOCK), lambda b, j: (0, j)),
                  pl.BlockSpec((half, MOBA_BLOCK), lambda b, j: (0, j))],
        out_specs=[pl.BlockSpec((1, MOBA_W, MOBA_BLOCK), lambda b, j: (b, 0, j)),
                   pl.BlockSpec((1, MOBA_HEADS, MOBA_BLOCK, MOBA_HD), lambda b, j: (b, 0, j, 0)),
                   pl.BlockSpec((1, 1, MOBA_HEADS * MOBA_VROWS, MOBA_BLOCK), lambda b, j: (b, j, 0, 0)),
                   pl.BlockSpec((1, 1, MOBA_HEADS, 16, MOBA_BLOCK), lambda b, j: (b, j, 0, 0, 0))],
        out_shape=[jax.ShapeDtypeStruct((B, MOBA_W, S), BF16),
                   jax.ShapeDtypeStruct((B, MOBA_HEADS, S, MOBA_HD), BF16),
                   jax.ShapeDtypeStruct((B, nb, MOBA_HEADS * MOBA_VROWS, MOBA_BLOCK), BF16),
                   jax.ShapeDtypeStruct((B, nb, MOBA_HEADS, 16, MOBA_BLOCK), F32)],
        scratch_shapes=[pltpu.VMEM((MOBA_HEADS * 16, MOBA_W), F32)],
        compiler_params=_params("arbitrary", "arbitrary"),
        name="moba_prep",
    )(proj, proj, projv, qw, kw, cos_t, sin_t)


def _moba_kernel(qt_ref, k_ref, vt_ref, bias_ref, o_ref, acc_ref, m_ref, s0_ref, s1_ref):
    j = pl.program_id(1)
    hd = MOBA_HD
    kpos = lax.broadcasted_iota(jnp.int32, (MOBA_BLOCK, MOBA_BLOCK), 0)
    qpos = lax.broadcasted_iota(jnp.int32, (MOBA_BLOCK, MOBA_BLOCK), 1)
    causal = kpos <= qpos
    own = pl.ds(pl.multiple_of(j * MOBA_BLOCK, MOBA_BLOCK), MOBA_BLOCK)
    heads = range(MOBA_HEADS)
    rows = [slice(h * hd, (h + 1) * hd) for h in heads]
    vrows = [slice(h * MOBA_VROWS, (h + 1) * MOBA_VROWS) for h in heads]
    s = [jnp.where(causal, jnp.dot(k_ref[0, h, own, :], qt_ref[0, rows[h], :], preferred_element_type=F32),
                   -jnp.inf) for h in heads]
    m = [jnp.max(s[h], axis=0, keepdims=True) for h in heads]
    p = [jnp.exp2(s[h] - m[h]) for h in heads]
    pv = [jnp.dot(vt_ref[0, j, vrows[h], :], p[h].astype(BF16), preferred_element_type=F32)
          for h in heads]
    for h in heads:
        m_ref[h] = m[h]
        acc_ref[h] = pv[h]

    nb = k_ref.shape[2] // MOBA_BLOCK

    def scores(n, s_ref):
        nc = jnp.minimum(n, nb - 1)
        blk = pl.ds(pl.multiple_of(nc * MOBA_BLOCK, MOBA_BLOCK), MOBA_BLOCK)
        for h in heads:
            s_ref[h] = (jnp.dot(k_ref[0, h, blk, :], qt_ref[0, rows[h], :], preferred_element_type=F32)
                        + bias_ref[0, 0, h, pl.ds(nc, 1), :])

    def consume(n, s_ref):
        s = {h: s_ref[h] for h in heads}
        m_old = {h: m_ref[h] for h in heads}
        m_new = {h: jnp.maximum(m_old[h], jnp.max(s[h], axis=0, keepdims=True)) for h in heads}
        alpha = {h: jnp.exp2(m_old[h] - m_new[h]) for h in heads}
        p = {h: jnp.exp2(s[h] - m_new[h]) for h in heads}
        pv = {h: jnp.dot(vt_ref[0, n, vrows[h], :], p[h].astype(BF16), preferred_element_type=F32)
              for h in heads}
        acc_new = {h: acc_ref[h] * alpha[h] + pv[h] for h in heads}
        for h in heads:
            m_ref[h] = m_new[h]
            acc_ref[h] = acc_new[h]

    scores(0, s0_ref)

    def pair(nn, carry):
        n = 2 * nn
        scores(n + 1, s1_ref)
        consume(n, s0_ref)
        scores(n + 2, s0_ref)
        consume(n + 1, s1_ref)
        return carry

    lax.fori_loop(0, j // 2, pair, 0)

    @pl.when(j % 2 == 1)
    def _():
        consume(j - 1, s0_ref)

    for h in range(MOBA_HEADS):
        o_ref[0, h * hd:(h + 1) * hd, :] = (acc_ref[h, 0:hd, :] / acc_ref[h, hd:hd + 1, :]).astype(BF16)


def _moba(qt, kh, vt, bias, B, S):
    nb = S // MOBA_BLOCK
    return pl.pallas_call(
        _moba_kernel,
        grid=(B, nb),
        in_specs=[pl.BlockSpec((1, MOBA_W, MOBA_BLOCK), lambda b, j: (b, 0, j)),
                  pl.BlockSpec((1, MOBA_HEADS, S, MOBA_HD), lambda b, j: (b, 0, 0, 0)),
                  pl.BlockSpec((1, nb, MOBA_HEADS * MOBA_VROWS, MOBA_BLOCK), lambda b, j: (b, 0, 0, 0)),
                  pl.BlockSpec((1, 1, MOBA_HEADS, 16, MOBA_BLOCK), lambda b, j: (b, j, 0, 0, 0))],
        out_specs=pl.BlockSpec((1, MOBA_W, MOBA_BLOCK), lambda b, j: (b, 0, j)),
        out_shape=jax.ShapeDtypeStruct((B, MOBA_W, S), BF16),
        scratch_shapes=[pltpu.VMEM((MOBA_HEADS, MOBA_VROWS, MOBA_BLOCK), F32),
                        pltpu.VMEM((MOBA_HEADS, 1, MOBA_BLOCK), F32),
                        pltpu.VMEM((MOBA_HEADS, MOBA_BLOCK, MOBA_BLOCK), F32),
                        pltpu.VMEM((MOBA_HEADS, MOBA_BLOCK, MOBA_BLOCK), F32)],
        compiler_params=_params("arbitrary", "arbitrary"),
        name="moba",
    )(qt, kh, vt, bias)


def _outproj_kernel(x_ref, og_ref, omt_ref, msm_ref, wo_ref, n2_ref, wq_ref, x1_ref, xn_ref, q_ref):
    om = (omt_ref[0].astype(F32).T * msm_ref[...]).astype(BF16)
    x1 = (x_ref[...]
          + jnp.dot(og_ref[...], wo_ref[0:GLA_V, :], preferred_element_type=F32)
          + jnp.dot(om, wo_ref[GLA_V:, :], preferred_element_type=F32))
    x1_ref[...] = x1
    ms = jnp.mean(x1 * x1, axis=-1, keepdims=True)
    xn = (x1 * lax.rsqrt(ms + EPS) * n2_ref[...]).astype(BF16)
    xn_ref[...] = xn
    q_ref[...] = jnp.dot(xn, wq_ref[...], preferred_element_type=F32).astype(BF16)


def _outproj(x2, o_gla, o_moba_t, ms_moba, w_out, norm2_w, wq, B, S):
    T = B * S
    nt = S // OUT_TILE
    QW = PEER_HEADS * PEER_QDIM
    return pl.pallas_call(
        _outproj_kernel,
        grid=(T // OUT_TILE,),
        in_specs=[pl.BlockSpec((OUT_TILE, D_MODEL), lambda i: (i, 0)),
                  pl.BlockSpec((OUT_TILE, GLA_V), lambda i: (i, 0)),
                  pl.BlockSpec((1, MOBA_W, OUT_TILE), lambda i: (i // nt, 0, i % nt)),
                  pl.BlockSpec((1, MOBA_W), lambda i: (0, 0)),
                  pl.BlockSpec((GLA_V + MOBA_W, D_MODEL), lambda i: (0, 0)),
                  pl.BlockSpec((1, D_MODEL), lambda i: (0, 0)),
                  pl.BlockSpec((D_MODEL, QW), lambda i: (0, 0))],
        out_specs=[pl.BlockSpec((OUT_TILE, D_MODEL), lambda i: (i, 0)),
                   pl.BlockSpec((OUT_TILE, D_MODEL), lambda i: (i, 0)),
                   pl.BlockSpec((OUT_TILE, QW), lambda i: (i, 0))],
        out_shape=[jax.ShapeDtypeStruct((T, D_MODEL), F32),
                   jax.ShapeDtypeStruct((T, D_MODEL), BF16),
                   jax.ShapeDtypeStruct((T, QW), BF16)],
        compiler_params=_params("arbitrary"),
        name="outproj",
    )(x2, o_gla, o_moba_t, ms_moba, w_out, norm2_w, wq)


_PAIRS = [(r, q) for r in range(PEER_TOPK + 1) for q in range(PEER_TOPK + 1)
          if (r + 1) * (q + 1) <= PEER_TOPK + 1]


def _tree(op, xs):
    xs = list(xs)
    while len(xs) > 1:
        xs = [op(xs[i], xs[i + 1]) if i + 1 < len(xs) else xs[i] for i in range(0, len(xs), 2)]
    return xs[0]


def _route_kernel(q_ref, sk_ref, rk_ref, e1_ref, cnt_ref, e0_ref, sc_ref, tmp_ref):
    H, NK = PEER_HEADS, PEER_NKEYS
    hq = H * PEER_QDIM // 2
    G = 8
    NTOP = PEER_TOPK + 1
    hd = PEER_QDIM // 2
    for p in range(2):
        for h in range(H):
            sc = _nt_dot(sk_ref[h, p], q_ref[:, (p * H + h) * hd:(p * H + h + 1) * hd])
            for lt in range(ROUTE_TILE // ROUTE_LANES):
                sc_ref[p, lt, pl.ds(h, NK, stride=H), :] = sc[:, lt * ROUTE_LANES:(lt + 1) * ROUTE_LANES]
    neg = jnp.full((H, ROUTE_LANES), -jnp.inf, F32)

    def insert(top, v):
        out = []
        for t in top:
            out.append(jnp.maximum(t, v))
            v = jnp.minimum(t, v)
        return out

    for lt in range(ROUTE_TILE // ROUTE_LANES):
        lanes = slice(lt * ROUTE_LANES, (lt + 1) * ROUTE_LANES)

        def largest(p, lt=lt):
            def body(g, top):
                v = sc_ref[p, lt, pl.ds(pl.multiple_of(g * (G * H), G * H), G * H), :]
                top = list(top)
                for k in range(G):
                    top = insert(top, v[k * H:(k + 1) * H])
                return tuple(top)
            return lax.fori_loop(0, NK // G, body, (neg,) * NTOP)

        tops = (largest(0), largest(1))
        best = [neg] * NTOP
        for r, q in _PAIRS:
            best = insert(best, tops[0][r] + tops[1][q])
        tau = 0.5 * (best[PEER_TOPK - 1] + best[PEER_TOPK])
        top = tops[0][0] + tops[1][0]
        zinv = 0.5 / _tree(jnp.add, [jnp.where(tops[0][r] + tops[1][q] > tau,
                                               jnp.exp(tops[0][r] + tops[1][q] - top), 0.0)
                                     for r, q in _PAIRS])
        thr = [tau - b for b in tops[1][:PEER_TOPK]]
        top1 = tops[1][:PEER_TOPK]

        def half0(g, carry, lt=lt, tops=tops, thr=thr, zinv=zinv):
            base = pl.multiple_of(g * (G * H), G * H)
            v0 = sc_ref[0, lt, pl.ds(base, G * H), :]
            for k in range(G):
                a = v0[k * H:(k + 1) * H]
                rows = pl.ds(pl.multiple_of(base + k * H, H), H)
                tmp_ref[0, rows, :] = _tree(jnp.add, [jnp.where(a > t, 1.0, 0.0) for t in thr])
                tmp_ref[1, rows, :] = jnp.exp(a - tops[0][0]) * zinv
            return carry

        def half1(g, carry, lt=lt, tops=tops, top1=top1):
            base = pl.multiple_of(g * (G * H), G * H)
            v1 = sc_ref[1, lt, pl.ds(base, G * H), :]
            for k in range(G):
                b = v1[k * H:(k + 1) * H]
                rows = pl.ds(pl.multiple_of(base + k * H, H), H)
                tmp_ref[2, rows, :] = 1.0 + _tree(jnp.add, [jnp.where(b < t, 1.0, 0.0) for t in top1])
                tmp_ref[3, rows, :] = jnp.exp(b - tops[1][0])
            return carry

        lax.fori_loop(0, NK // G, half0, 0)
        lax.fori_loop(0, NK // G, half1, 0)
        for h in range(H):
            cnt_ref[h, :, lanes] = tmp_ref[0, pl.ds(h, NK, stride=H), :]
            e0_ref[h, :, lanes] = tmp_ref[1, pl.ds(h, NK, stride=H), :]
            rk_ref[h, :, lanes] = tmp_ref[2, pl.ds(h, NK, stride=H), :].astype(BF16)
            e1_ref[h, :, lanes] = tmp_ref[3, pl.ds(h, NK, stride=H), :].astype(BF16)


def _route(q, subkeys):
    T = q.shape[0]
    H, NK = PEER_HEADS, PEER_NKEYS
    hq = H * PEER_QDIM // 2
    spec = pl.BlockSpec((H, NK, ROUTE_TILE), lambda i: (0, 0, i))
    return pl.pallas_call(
        _route_kernel,
        grid=(T // ROUTE_TILE,),
        in_specs=[pl.BlockSpec((ROUTE_TILE, 2 * hq), lambda i: (i, 0)),
                  pl.BlockSpec((H, 2, NK, PEER_QDIM // 2), lambda i: (0, 0, 0, 0))],
        out_specs=[spec, spec, spec, spec],
        out_shape=[jax.ShapeDtypeStruct((H, NK, T), BF16), jax.ShapeDtypeStruct((H, NK, T), BF16),
                   jax.ShapeDtypeStruct((H, NK, T), F32), jax.ShapeDtypeStruct((H, NK, T), F32)],
        scratch_shapes=[pltpu.VMEM((2, ROUTE_TILE // ROUTE_LANES, NK * H, ROUTE_LANES), F32),
                        pltpu.VMEM((4, NK * H, ROUTE_LANES), F32)],
        compiler_params=_params("arbitrary"),
        name="peer_route",
    )(q, subkeys)


PEER_NCHUNK = PEER_N // PEER_ECHUNK


def _peer_kernel(xn_ref, x1_ref, u_ref, vt_ref, rk_ref, e1_ref, cnt_ref, e0_ref, o_ref,
                 acc_ref, act_ref, w_ref):
    c = pl.program_id(1)
    NK = PEER_NKEYS
    nsub = PEER_ECHUNK // NK
    RB = PEER_STRIP_ROWS

    @pl.when(c == 0)
    def _():
        acc_ref[...] = jnp.zeros_like(acc_ref)

    bounds = (0,) + PEER_ACT_SPLITS + (PEER_ECHUNK,)
    for ii in range(nsub):
        if ii * NK in bounds[:-1]:
            hs = slice(ii * NK, bounds[bounds.index(ii * NK) + 1])
            act_ref[hs, :] = _nt_dot(u_ref[hs, :], xn_ref[...])
        i = c * nsub + ii
        cnt_rows = [cnt_ref[h, pl.ds(i, 1), :] for h in range(PEER_HEADS)]
        e0_rows = [e0_ref[h, pl.ds(i, 1), :] for h in range(PEER_HEADS)]
        for lc in range(PEER_TILE // LANES):
            lanes = slice(lc * LANES, (lc + 1) * LANES)
            cnts = [jnp.broadcast_to(r[:, lanes], (RB, LANES)).astype(BF16) for r in cnt_rows]
            e0s = [jnp.broadcast_to(r[:, lanes], (RB, LANES)).astype(BF16) for r in e0_rows]
            for rb in range(NK // RB):
                rows = slice(rb * RB, (rb + 1) * RB)
                wsum = _tree(jnp.add, [
                    e0s[h] * jnp.where(rk_ref[h, rows, lanes] <= cnts[h], e1_ref[h, rows, lanes], 0.0)
                    for h in range(PEER_HEADS)])
                er = slice(ii * NK + rb * RB, ii * NK + (rb + 1) * RB)
                a = act_ref[er, lanes]
                g2 = a * (1.0 + lax.erf(a * (1.0 / math.sqrt(2.0))))
                w_ref[er, lanes] = g2.astype(BF16) * wsum
    acc_ref[...] += jnp.dot(vt_ref[0], w_ref[...], preferred_element_type=F32)

    @pl.when(c == PEER_NCHUNK - 1)
    def _():
        o_ref[...] = x1_ref[...] + acc_ref[...].T


def _peer(xn, x1, u, vt, rk, e1, cnt, e0):
    T = xn.shape[0]
    H, NK = PEER_HEADS, PEER_NKEYS
    rspec = pl.BlockSpec((H, NK, PEER_TILE), lambda t, c: (0, 0, t))
    return pl.pallas_call(
        _peer_kernel,
        grid=(T // PEER_TILE, PEER_NCHUNK),
        in_specs=[pl.BlockSpec((PEER_TILE, D_MODEL), lambda t, c: (t, 0)),
                  pl.BlockSpec((PEER_TILE, D_MODEL), lambda t, c: (t, 0)),
                  pl.BlockSpec((PEER_ECHUNK, D_MODEL), lambda t, c: (c, 0)),
                  pl.BlockSpec((1, D_MODEL, PEER_ECHUNK), lambda t, c: (c, 0, 0)),
                  rspec, rspec, rspec, rspec],
        out_specs=pl.BlockSpec((PEER_TILE, D_MODEL), lambda t, c: (t, 0)),
        out_shape=jax.ShapeDtypeStruct((T, D_MODEL), F32),
        scratch_shapes=[pltpu.VMEM((D_MODEL, PEER_TILE), F32),
                        pltpu.VMEM((PEER_ECHUNK, PEER_TILE), F32),
                        pltpu.VMEM((PEER_ECHUNK, PEER_TILE), BF16)],
        compiler_params=_params("arbitrary", "arbitrary"),
        name="peer",
    )(xn, x1, u, vt, rk, e1, cnt, e0)


def _layer(x, norm1_w, w_in, w_alpha, b_alpha, out_norm_w, qn_w, kn_w, mix_scale, w_out,
           norm2_w, w_query, subkeys, u_tab, v_tab):
    B, S, D = x.shape
    T = B * S
    x2 = x.reshape(T, D)

    o_gv = 2 * GLA_QK
    o_gate = o_gv + GLA_V
    o_gr = o_gate + GLA_V
    o_mq = o_gr + GLA_RANK
    o_mv = o_mq + 2 * MOBA_W
    w_r = jnp.concatenate([w_in[:, :o_gv], w_in[:, o_gate:o_gr], w_in[:, o_mq:o_mv], w_in[:, o_gr:o_mq],
                           jnp.zeros((D, 128 - GLA_RANK), w_in.dtype)], axis=1).astype(BF16)
    w_v = jnp.concatenate([w_in[:, o_gv:o_gate], w_in[:, o_mv:]], axis=1).astype(BF16)
    wa = jnp.concatenate([w_alpha, jnp.zeros((128 - GLA_RANK, GLA_QK), w_alpha.dtype)], axis=0)

    proj, projv = _inproj(x2, norm1_w.reshape(1, D), w_r, w_v)
    o_gla = _gla(proj, projv, B, S, wa, b_alpha.reshape(1, GLA_QK), out_norm_w.reshape(1, GLA_DV),
                 mix_scale[:GLA_V].reshape(1, GLA_V))

    half = MOBA_HD // 2
    inv = ROPE_THETA ** (-jnp.arange(half, dtype=F32) / half)
    ang = inv[:, None] * jnp.arange(S, dtype=F32)[None, :]
    qw = jnp.broadcast_to(qn_w.reshape(MOBA_HD, 1), (MOBA_HD, MOBA_BLOCK))
    kw = jnp.broadcast_to(kn_w.reshape(MOBA_HD, 1), (MOBA_HD, MOBA_BLOCK))
    qt, kh, vt, bias = _moba_prep(proj, projv, B, S, qw, kw, jnp.cos(ang), jnp.sin(ang))
    o_moba_t = _moba(qt, kh, vt, bias, B, S)

    H, hd = PEER_HEADS, PEER_QDIM // 2
    wq = w_query.reshape(D, H, 2, hd).transpose(0, 2, 1, 3).reshape(D, 2 * H * hd).astype(BF16)

    x1, xn, q = _outproj(x2, o_gla, o_moba_t, mix_scale[GLA_V:].reshape(1, MOBA_W),
                         w_out.astype(BF16), norm2_w.reshape(1, D), wq, B, S)
    rk, e1, cnt, e0 = _route(q, subkeys.astype(BF16))
    vt = v_tab.reshape(PEER_NCHUNK, PEER_ECHUNK, D).transpose(0, 2, 1).astype(BF16)
    out = _peer(xn, x1, u_tab.astype(BF16), vt, rk, e1, cnt, e0)
    return out.reshape(B, S, D)


def kernel(x, norm1_w, w_in, gla_w_alpha, gla_b_alpha, gla_out_norm_w, moba_q_norm_w, moba_k_norm_w,
           mix_scale, w_out, norm2_w, peer_w_query, peer_subkeys, peer_u, peer_v):
    assert norm1_w.shape[0] == 1, "single-layer kernel"
    return _layer(x, norm1_w[0], w_in[0], gla_w_alpha[0], gla_b_alpha[0], gla_out_norm_w[0],
                  moba_q_norm_w[0], moba_k_norm_w[0], mix_scale[0], w_out[0], norm2_w[0],
                  peer_w_query[0], peer_subkeys[0], peer_u[0], peer_v[0])
```
